```python
import math
import jax
import jax.numpy as jnp
from jax import lax
import numpy as np

D_MODEL = 1024
BATCH = 2
SEQ = 8192
DEPTH = 4

N_MIXERS = 4
HEAD_DIM = 64
RMS_EPS = 1e-6
NEG_INF = -1e30
TINY = 1e-30
Q_BLOCK = 128

A_HEADS = D_MODEL // HEAD_DIM
A_KV_HEADS = A_HEADS // 4
A_GROUP = A_HEADS // A_KV_HEADS
A_CMP_BLOCK = 32
A_CMP_STRIDE = 16
A_CMP_HIDDEN = 4 * HEAD_DIM
A_SEL_BLOCK = 64
A_SEL_TOPK = 16
A_WINDOW = 512
A_Q_BLOCK = 64
A_FORCE_BONUS = 1e3
A_SPLITS = (A_HEADS * HEAD_DIM,) + (A_KV_HEADS * HEAD_DIM,) * 6 + (3 * A_HEADS, D_MODEL)
A_IN = sum(A_SPLITS)

B_HEADS = D_MODEL // (2 * HEAD_DIM)
B_IN = 4 * D_MODEL

C_PATTERNS = ((128, 1), (512, 4), (2048, 16))
C_GROUPS = 3
C_HEADS = 4
C_V_DIM = D_MODEL // C_HEADS
C_Q_BLOCK = 64
C_IN = 2 * C_GROUPS * C_HEADS * HEAD_DIM + 2 * D_MODEL

D_HEADS = D_MODEL // HEAD_DIM
D_IN = 4 * D_MODEL

kernel_name = 'hybrid_nsa_diff_dilated_stickbreak_trunk'


def rms_norm(x, gain):
    xf = x.astype(jnp.float32)
    y = xf * lax.rsqrt(jnp.mean(xf * xf, axis=-1, keepdims=True) + RMS_EPS)
    return (y * gain.astype(jnp.float32)).astype(x.dtype)


def alibi_slopes(n):
    return jnp.asarray(np.array([2.0 ** (-8.0 * (i + 1) / n) for i in range(n)], np.float32))


def split_cols(a, sizes):
    return jnp.split(a, [int(c) for c in np.cumsum(sizes)[:-1]], axis=-1)


def masked_softmax(s, mask):
    s = jnp.where(mask, s, NEG_INF)
    e = jnp.where(mask, jnp.exp(s - jnp.max(s, axis=-1, keepdims=True)), 0.0)
    return e / jnp.maximum(jnp.sum(e, axis=-1, keepdims=True), TINY)


def unblock(o):
    nq, b, h, t, d = o.shape
    return o.transpose(1, 0, 3, 2, 4).reshape(b, nq * t, h * d)


def nsa_mixer(u, w_in, w_out, pos_k, pos_v, w1_k, w2_k, w1_v, w2_v):
    B, S, _ = u.shape
    H, Hk, G, dh = A_HEADS, A_KV_HEADS, A_GROUP, HEAD_DIM
    L, st, SB, W, TQ = A_CMP_BLOCK, A_CMP_STRIDE, A_SEL_BLOCK, A_WINDOW, A_Q_BLOCK
    f32 = jnp.float32
    q, kc, vc, ks, vs, kw, vw, gl, z = split_cols(u @ w_in, A_SPLITS)
    q = q.reshape(B, S, Hk, G, dh).transpose(0, 2, 3, 1, 4).astype(f32) * dh ** -0.5

    def heads(a):
        return a.reshape(B, S, Hk, dh).transpose(0, 2, 1, 3).astype(f32)

    kc, vc, ks, vs, kw, vw = (heads(a) for a in (kc, vc, ks, vs, kw, vw))

    n_cmp = (S - L) // st + 1
    cidx = st * np.arange(n_cmp)[:, None] + np.arange(L)[None, :]
    cmp_end = jnp.asarray(cidx[:, -1])

    def compress(a, pos, w1, w2):
        blk = (a[:, :, cidx] + pos).reshape(B, Hk, n_cmp, L * dh)
        return jax.nn.gelu(blk @ w1) @ w2

    kcmp = compress(kc, pos_k, w1_k, w2_k).astype(f32)
    vcmp = compress(vc, pos_v, w1_v, w2_v).astype(f32)

    n_sel = S // SB
    ov = np.zeros((n_cmp, n_sel), np.float32)
    np.add.at(ov, (np.repeat(np.arange(n_cmp), L), (cidx // SB).ravel()), 1.0 / L)
    ov = jnp.asarray(ov)
    ks_blk = ks.reshape(B, Hk, n_sel, SB, dh)
    vs_blk = vs.reshape(B, Hk, n_sel, SB, dh)
    topk = min(A_SEL_TOPK, n_sel)
    bi = jnp.arange(B)[:, None, None, None]
    hi = jnp.arange(Hk)[None, :, None, None]

    kw_p = jnp.pad(kw, ((0, 0), (0, 0), (W, 0), (0, 0)))
    vw_p = jnp.pad(vw, ((0, 0), (0, 0), (W, 0), (0, 0)))
    slopes = alibi_slopes(H).reshape(1, Hk, G, 1, 1)

    def block(i):
        q0 = i * TQ
        t = q0 + jnp.arange(TQ)
        qb = lax.dynamic_slice_in_dim(q, q0, TQ, axis=3)
        dist = t[:, None] - cmp_end[None, :]
        p_c = masked_softmax(jnp.einsum('bhgqd,bhnd->bhgqn', qb, kcmp) - slopes * dist.astype(f32), dist >= 0)
        o_c = jnp.einsum('bhgqn,bhnd->bhgqd', p_c, vcmp)
        imp = jnp.einsum('bhgqn,nj->bhqj', p_c, ov)
        cur = (t // SB)[:, None]
        j = jnp.arange(n_sel)[None, :]
        forced = (j == 0) | (j == cur) | (j == cur - 1)
        imp = jnp.where(j > cur, -1.0, imp + jnp.where(forced, A_FORCE_BONUS, 0.0))
        _, sel = lax.top_k(imp, topk)
        ksel = ks_blk[bi, hi, sel].reshape(B, Hk, TQ, topk * SB, dh)
        vsel = vs_blk[bi, hi, sel].reshape(B, Hk, TQ, topk * SB, dh)
        pos = (sel[..., None] * SB + jnp.arange(SB)).reshape(B, Hk, TQ, topk * SB)
        dist = (t[:, None] - pos)[:, :, None]
        p_s = masked_softmax(jnp.einsum('bhgqd,bhqkd->bhgqk', qb, ksel) - slopes * dist.astype(f32), dist >= 0)
        o_s = jnp.einsum('bhgqk,bhqkd->bhgqd', p_s, vsel)
        kwin = lax.dynamic_slice_in_dim(kw_p, q0, TQ + W, axis=2)
        vwin = lax.dynamic_slice_in_dim(vw_p, q0, TQ + W, axis=2)
        spos = q0 - W + jnp.arange(TQ + W)
        dist = t[:, None] - spos[None, :]
        mask = (dist >= 0) & (dist < W) & (spos[None, :] >= 0)
        p_w = masked_softmax(jnp.einsum('bhgqd,bhkd->bhgqk', qb, kwin) - slopes * dist.astype(f32), mask)
        o_w = jnp.einsum('bhgqk,bhkd->bhgqd', p_w, vwin)
        return jnp.stack([o_c, o_s, o_w])

    outs = lax.map(block, jnp.arange(S // TQ))
    outs = outs.transpose(1, 2, 0, 5, 3, 4, 6).reshape(3, B, S, H, dh)
    gates = jax.nn.sigmoid(gl.astype(f32)).reshape(B, S, 3, H).transpose(2, 0, 1, 3)[..., None]
    o = jnp.sum(gates * outs, axis=0).reshape(B, S, H * dh)
    return (o.astype(u.dtype) * jax.nn.silu(z)) @ w_out


def diff_mixer(u, w_in, w_out, lam, sub_gain, lambda_init):
    B, S, _ = u.shape
    H, dh, TQ = B_HEADS, HEAD_DIM, Q_BLOCK
    f32 = jnp.float32
    q, k, v, z = split_cols(u @ w_in, (D_MODEL,) * 4)
    q = q.reshape(B, S, H, 2, dh).transpose(0, 2, 3, 1, 4).astype(f32) * dh ** -0.5
    k = k.reshape(B, S, H, 2, dh).transpose(0, 2, 3, 1, 4).astype(f32)
    v = v.reshape(B, S, H, 2 * dh).transpose(0, 2, 1, 3).astype(f32)
    lamf = lam.astype(f32)
    lam_full = jnp.exp(jnp.sum(lamf[0] * lamf[1])) - jnp.exp(jnp.sum(lamf[2] * lamf[3])) + lambda_init
    slopes = alibi_slopes(H).reshape(1, H, 1, 1, 1)
    kpos = jnp.arange(S)

    def block(i):
        q0 = i * TQ
        t = q0 + jnp.arange(TQ)
        qb = lax.dynamic_slice_in_dim(q, q0, TQ, axis=3)
        dist = t[:, None] - kpos[None, :]
        p = masked_softmax(jnp.einsum('bhmqd,bhmkd->bhmqk', qb, k) - slopes * dist.astype(f32), dist >= 0)
        a = p[:, :, 0] - lam_full * p[:, :, 1]
        return jnp.einsum('bhqk,bhkd->bhqd', a, v)

    o = lax.map(block, jnp.arange(S // TQ))
    o = o.transpose(1, 0, 3, 2, 4).reshape(B, S, H, 2 * dh)
    o = (rms_norm(o, sub_gain) * (1.0 - lambda_init)).reshape(B, S, H * 2 * dh)
    return (o.astype(u.dtype) * jax.nn.silu(z)) @ w_out


def dilated_mixer(u, w_in, w_out):
    B, S, _ = u.shape
    G, Hg, dh, dv, TQ = C_GROUPS, C_HEADS, HEAD_DIM, C_V_DIM, C_Q_BLOCK
    f32 = jnp.float32
    qk = G * Hg * dh
    q, k, v, z = split_cols(u @ w_in, (qk, qk, D_MODEL, D_MODEL))
    q = q.reshape(B, S, G, Hg, dh).transpose(2, 0, 3, 1, 4).astype(f32) * dh ** -0.5
    k = k.reshape(B, S, G, Hg, dh).transpose(2, 0, 3, 1, 4).astype(f32)
    v = v.reshape(B, S, Hg, dv).transpose(0, 2, 1, 3).astype(f32)
    slopes = alibi_slopes(G * Hg).reshape(G, Hg)

    def block(i):
        q0 = i * TQ
        t = q0 + jnp.arange(TQ)
        outs, lses = [], []
        for g, (w, d) in enumerate(C_PATTERNS):
            offs = d * np.arange(w // d + 1)
            p = t[:, None] - offs[None, :]
            valid = p >= 0
            pc = jnp.maximum(p, 0)
            kg = jnp.take(k[g], pc, axis=2)
            vg = jnp.take(v, pc, axis=2)
            qb = lax.dynamic_slice_in_dim(q[g], q0, TQ, axis=2)
            s = jnp.einsum('bhqd,bhqnd->bhqn', qb, kg) - slopes[g][None, :, None, None] * offs.astype(np.float32)
            s = jnp.where(valid, s, NEG_INF)
            m = jnp.max(s, axis=-1, keepdims=True)
            e = jnp.where(valid, jnp.exp(s - m), 0.0)
            l = jnp.sum(e, axis=-1, keepdims=True)
            outs.append(jnp.einsum('bhqn,bhqnd->bhqd', e / l, vg))
            lses.append(m[..., 0] + jnp.log(l[..., 0]))
        wts = jax.nn.softmax(jnp.stack(lses), axis=0)[..., None]
        return jnp.sum(wts * jnp.stack(outs), axis=0)

    o = unblock(lax.map(block, jnp.arange(S // TQ)))
    return (o.astype(u.dtype) * jax.nn.silu(z)) @ w_out


def stick_mixer(u, w_in, w_out):
    B, S, _ = u.shape
    H, dh, TQ = D_HEADS, HEAD_DIM, Q_BLOCK
    f32 = jnp.float32
    q, k, v, z = split_cols(u @ w_in, (D_MODEL,) * 4)
    q = q.reshape(B, S, H, dh).transpose(0, 2, 1, 3).astype(f32) * dh ** -0.5
    k = k.reshape(B, S, H, dh).transpose(0, 2, 1, 3).astype(f32)
    v = v.reshape(B, S, H, dh).transpose(0, 2, 1, 3).astype(f32)
    kpos = jnp.arange(S)

    def block(i):
        q0 = i * TQ
        t = q0 + jnp.arange(TQ)
        qb = lax.dynamic_slice_in_dim(q, q0, TQ, axis=2)
        logits = jnp.einsum('bhqd,bhkd->bhqk', qb, k)
        causal = kpos[None, :] < t[:, None]
        log_om = jnp.where(causal, jax.nn.log_sigmoid(-logits), 0.0)
        between = lax.cumsum(log_om, axis=3, reverse=True) - log_om
        a = jnp.where(causal, jnp.exp(jax.nn.log_sigmoid(logits) + between), 0.0)
        return jnp.einsum('bhqk,bhkd->bhqd', a, v)

    o = unblock(lax.map(block, jnp.arange(S // TQ)))
    return (o.astype(u.dtype) * jax.nn.silu(z)) @ w_out


def setup_inputs(seed: int = 0) -> dict:
    key = jax.random.key(seed)
    ks = jax.random.split(key, 20)
    n_a = (DEPTH + 3) // 4
    n_b = (DEPTH + 2) // 4
    n_c = (DEPTH + 1) // 4
    n_d = DEPTH // 4
    L, dh = A_CMP_BLOCK, HEAD_DIM

    def dense(k, shape, fan_in):
        return jax.random.normal(k, shape, jnp.float32) * fan_in ** -0.5

    def gain(k, shape):
        return 1.0 + 0.02 * jax.random.normal(k, shape, jnp.float32)

    return {
        'x': jax.random.normal(ks[0], (BATCH, SEQ, D_MODEL), jnp.float32),
        'norm_pre': gain(ks[1], (DEPTH, D_MODEL)),
        'norm_post': gain(ks[2], (DEPTH, D_MODEL)),
        'a_w_in': dense(ks[3], (n_a, D_MODEL, A_IN), D_MODEL),
        'a_w_out': dense(ks[4], (n_a, D_MODEL, D_MODEL), D_MODEL),
        'a_cmp_pos_k': 0.1 * jax.random.normal(ks[5], (n_a, L, dh), jnp.float32),
        'a_cmp_pos_v': 0.1 * jax.random.normal(ks[6], (n_a, L, dh), jnp.float32),
        'a_cmp_w1_k': dense(ks[7], (n_a, L * dh, A_CMP_HIDDEN), L * dh),
        'a_cmp_w2_k': dense(ks[8], (n_a, A_CMP_HIDDEN, dh), A_CMP_HIDDEN),
        'a_cmp_w1_v': dense(ks[9], (n_a, L * dh, A_CMP_HIDDEN), L * dh),
        'a_cmp_w2_v': dense(ks[10], (n_a, A_CMP_HIDDEN, dh), A_CMP_HIDDEN),
        'b_w_in': dense(ks[11], (n_b, D_MODEL, B_IN), D_MODEL),
        'b_w_out': dense(ks[12], (n_b, D_MODEL, D_MODEL), D_MODEL),
        'b_lambda': 0.1 * jax.random.normal(ks[13], (n_b, 4, dh), jnp.float32),
        'b_sub_gain': gain(ks[14], (n_b, 2 * dh)),
        'c_w_in': dense(ks[15], (n_c, D_MODEL, C_IN), D_MODEL),
        'c_w_out': dense(ks[16], (n_c, D_MODEL, D_MODEL), D_MODEL),
        'd_w_in': dense(ks[17], (n_d, D_MODEL, D_IN), D_MODEL),
        'd_w_out': dense(ks[18], (n_d, D_MODEL, D_MODEL), D_MODEL),
    }


def reference(x, norm_pre, norm_post, a_w_in, a_w_out, a_cmp_pos_k, a_cmp_pos_v, a_cmp_w1_k, a_cmp_w2_k,
              a_cmp_w1_v, a_cmp_w2_v, b_w_in, b_w_out, b_lambda, b_sub_gain, c_w_in, c_w_out, d_w_in, d_w_out):
    h = x
    for i in range(DEPTH):
        m, j = i % N_MIXERS, i // N_MIXERS
        u = rms_norm(h, norm_pre[i])
        if m == 0:
            y = nsa_mixer(u, a_w_in[j], a_w_out[j], a_cmp_pos_k[j], a_cmp_pos_v[j],
                          a_cmp_w1_k[j], a_cmp_w2_k[j], a_cmp_w1_v[j], a_cmp_w2_v[j])
        elif m == 1:
            lambda_init = 0.8 - 0.6 * math.exp(-0.3 * i)
            y = diff_mixer(u, b_w_in[j], b_w_out[j], b_lambda[j], b_sub_gain[j], lambda_init)
        elif m == 2:
            y = dilated_mixer(u, c_w_in[j], c_w_out[j])
        else:
            y = stick_mixer(u, d_w_in[j], d_w_out[j])
        h = h + rms_norm(y, norm_post[i])
    return h
```

```python
import functools
import math

import jax
import jax.numpy as jnp
import numpy as np
from jax import lax
from jax.experimental import pallas as pl
from jax.experimental.pallas import tpu as pltpu

F32 = jnp.float32
BF16 = jnp.bfloat16

D_MODEL = 1024
HEAD_DIM = 64
DEPTH = 4
RMS_EPS = 1e-6
NEG_INF = -1e30
TINY = 1e-30
LANES = 128
VMEM_LIMIT = 48 * 1024 * 1024

A_HEADS = 16
A_KV_HEADS = 4
A_GROUP = 4
A_CMP_BLOCK = 32
A_CMP_STRIDE = 16
A_CMP_HIDDEN = 256
A_SEL_BLOCK = 64
A_SEL_SHIFT = 6
A_SEL_TOPK = 16
A_WINDOW = 512
A_FORCE_BONUS = 1e3
A_TQ = 128
A_TK = 256
B_HEADS = 8
B_T = 256
C_PATTERNS = ((128, 1), (512, 4), (2048, 16))
C_HEADS = 4
C_V_DIM = 256
C_TQ = 128
D_HEADS = 16
D_T = 256

PROJ_TM = 512
PROJ_CHUNK = 512

_NT = (((1,), (1,)), ((), ()))


def _alibi_slopes(n):
    return np.array([2.0 ** (-8.0 * (i + 1) / n) for i in range(n)], np.float32)


def _params(n_grid):
    return pltpu.CompilerParams(dimension_semantics=("arbitrary",) * n_grid,
                                vmem_limit_bytes=VMEM_LIMIT)


def _split_bf16(x):
    hi = x.astype(BF16)
    lo = (x - hi.astype(F32)).astype(BF16)
    return hi, lo


def _dot_split(x, w):
    hi, lo = _split_bf16(x)
    return (jnp.dot(hi, w, preferred_element_type=F32) + jnp.dot(lo, w, preferred_element_type=F32))


def _sigmoid(x):
    return 1.0 / (1.0 + jnp.exp(-x))


def _proj_in_body(h_ref, g_ref, w_ref, o_ref):
    x = h_ref[...]
    ms = jnp.mean(x * x, axis=-1, keepdims=True)
    u = (x * lax.rsqrt(ms + RMS_EPS) * g_ref[...]).astype(BF16)
    n = o_ref.shape[1]
    for c in range(0, n, PROJ_CHUNK):
        e = min(c + PROJ_CHUNK, n)
        o_ref[:, c:e] = jnp.dot(u, w_ref[:, c:e], preferred_element_type=F32).astype(BF16)


def _proj_in(h, gain, w):
    m, n = h.shape[0], w.shape[1]
    return pl.pallas_call(
        _proj_in_body,
        grid=(m // PROJ_TM,),
        in_specs=[pl.BlockSpec((PROJ_TM, D_MODEL), lambda i: (i, 0)),
                  pl.BlockSpec((1, D_MODEL), lambda i: (0, 0)),
                  pl.BlockSpec((D_MODEL, n), lambda i: (0, 0))],
        out_specs=pl.BlockSpec((PROJ_TM, n), lambda i: (i, 0)),
        out_shape=jax.ShapeDtypeStruct((m, n), BF16),
        compiler_params=_params(1),
        name="proj_in",
    )(h, gain.reshape(1, D_MODEL), w)


def _finish(o, z_ref, w_ref, g_ref, h_ref, out_ref):
    z = z_ref[...].astype(F32)
    gated = (o * (z * _sigmoid(z))).astype(BF16)
    y = jnp.dot(gated, w_ref[...], preferred_element_type=F32)
    ms = jnp.mean(y * y, axis=-1, keepdims=True)
    out_ref[...] = h_ref[...] + y * lax.rsqrt(ms + RMS_EPS) * g_ref[...]


def _proj_out_body(o_ref, z_ref, w_ref, g_ref, h_ref, out_ref):
    _finish(o_ref[...].astype(F32), z_ref, w_ref, g_ref, h_ref, out_ref)


def _proj_out_nsa_body(oc_ref, os_ref, ow_ref, gl_ref, eg_ref, z_ref, w_ref, g_ref, h_ref, out_ref):
    gates = _dot_split(_sigmoid(gl_ref[...].astype(F32)), eg_ref[...])
    o = (gates[:, :D_MODEL] * oc_ref[...].astype(F32)
         + gates[:, D_MODEL:2 * D_MODEL] * os_ref[...].astype(F32)
         + gates[:, 2 * D_MODEL:] * ow_ref[...].astype(F32))
    _finish(o, z_ref, w_ref, g_ref, h_ref, out_ref)


def _proj_out(o_list, proj, z_blk, w_out, gain, h, gl_blk=None):
    m = h.shape[0]
    tm = PROJ_TM
    row = lambda i: (i, 0)
    fixed = lambda i: (0, 0)
    o_specs = [pl.BlockSpec((tm, D_MODEL), row) for _ in o_list]
    tail_specs = [pl.BlockSpec((tm, D_MODEL), lambda i: (i, z_blk)),
                  pl.BlockSpec((D_MODEL, D_MODEL), fixed),
                  pl.BlockSpec((1, D_MODEL), fixed),
                  pl.BlockSpec((tm, D_MODEL), row)]
    tail = [proj, w_out, gain.reshape(1, D_MODEL), h]
    if gl_blk is None:
        body, in_specs, args = _proj_out_body, o_specs + tail_specs, list(o_list) + tail
    else:
        eg = np.zeros((LANES, 3 * D_MODEL), np.float32)
        for br in range(3):
            for hd in range(A_HEADS):
                c0 = br * D_MODEL + hd * HEAD_DIM
                eg[br * A_HEADS + hd, c0:c0 + HEAD_DIM] = 1.0
        body = _proj_out_nsa_body
        in_specs = o_specs + [pl.BlockSpec((tm, LANES), lambda i: (i, gl_blk)),
                              pl.BlockSpec((LANES, 3 * D_MODEL), fixed)] + tail_specs
        args = list(o_list) + [proj, jnp.asarray(eg, BF16)] + tail
    return pl.pallas_call(
        body,
        grid=(m // tm,),
        in_specs=in_specs,
        out_specs=pl.BlockSpec((tm, D_MODEL), row),
        out_shape=jax.ShapeDtypeStruct((m, D_MODEL), F32),
        compiler_params=_params(1),
        name="proj_out",
    )(*args)


def _half_masked(q32):
    lane = lax.broadcasted_iota(jnp.int32, (1, LANES), 1)
    lo = jnp.where(lane < HEAD_DIM, q32, 0.0).astype(BF16)
    hi = jnp.where(lane >= HEAD_DIM, q32, 0.0).astype(BF16)
    return lo, hi


def _smem_spec():
    return pl.BlockSpec(memory_space=pltpu.SMEM)


def _diff_body(slopes_ref, q_ref, k_ref, v_ref, lam_ref, sg_ref, o_ref, m_sc, l_sc, acc_sc, *, lambda_init):
    t = B_T
    hd = pl.program_id(1)
    i = pl.program_id(2)
    slope = slopes_ref[hd]
    qm = _half_masked(q_ref[...].astype(F32) * HEAD_DIM ** -0.5)
    m_sc[...] = jnp.full(m_sc.shape, NEG_INF, F32)
    l_sc[...] = jnp.zeros(l_sc.shape, F32)
    acc_sc[...] = jnp.zeros(acc_sc.shape, F32)
    kl = lax.broadcasted_iota(jnp.int32, (1, t), 1)
    tl = lax.broadcasted_iota(jnp.int32, (t, 1), 0)

    def step(j, diagonal):
        k0 = pl.multiple_of(j * t, t)
        k = k_ref[pl.ds(k0, t), :]
        v = v_ref[pl.ds(k0, t), :]
        bias = slope * (kl + (j - i) * t).astype(F32)
        for mp in range(2):
            s = lax.dot_general(qm[mp], k, _NT, preferred_element_type=F32) + bias
            if diagonal:
                s = jnp.where(kl <= tl, s, NEG_INF)
            m_old = m_sc[mp]
            m_new = jnp.maximum(m_old, jnp.max(s, axis=1, keepdims=True))
            p = jnp.exp(s - m_new)
            alpha = jnp.exp(m_old - m_new)
            l_sc[mp] = alpha * l_sc[mp] + jnp.sum(p, axis=1, keepdims=True)
            acc_sc[mp] = alpha * acc_sc[mp] + jnp.dot(p.astype(BF16), v, preferred_element_type=F32)
            m_sc[mp] = m_new

    def body(j, carry):
        step(j, False)
        return carry

    lax.fori_loop(0, i, body, 0)
    step(i, True)

    lam = lam_ref[...]
    lam_full = (jnp.exp(jnp.sum(lam[0:1] * lam[1:2], axis=1, keepdims=True))
                - jnp.exp(jnp.sum(lam[2:3] * lam[3:4], axis=1, keepdims=True)) + lambda_init)
    o0 = acc_sc[0] / jnp.maximum(l_sc[0], TINY)
    o1 = acc_sc[1] / jnp.maximum(l_sc[1], TINY)
    a = o0 - lam_full * o1
    ms = jnp.mean(a * a, axis=-1, keepdims=True)
    o_ref[...] = ((a * lax.rsqrt(ms + RMS_EPS) * sg_ref[...]) * (1.0 - lambda_init)).astype(BF16)


def _diff_attention(proj, lam, sub_gain, lambda_init, batch, seq):
    t = B_T
    nq = seq // t
    nh = B_HEADS
    return pl.pallas_call(
        functools.partial(_diff_body, lambda_init=lambda_init),
        grid=(batch, nh, nq),
        in_specs=[_smem_spec(),
                  pl.BlockSpec((t, LANES), lambda b, h, i: (b * nq + i, h)),
                  pl.BlockSpec((seq, LANES), lambda b, h, i: (b, nh + h)),
                  pl.BlockSpec((seq, LANES), lambda b, h, i: (b, 2 * nh + h)),
                  pl.BlockSpec((4, HEAD_DIM), lambda b, h, i: (0, 0)),
                  pl.BlockSpec((1, LANES), lambda b, h, i: (0, 0))],
        out_specs=pl.BlockSpec((t, LANES), lambda b, h, i: (b * nq + i, h)),
        out_shape=jax.ShapeDtypeStruct((batch * seq, D_MODEL), BF16),
        scratch_shapes=[pltpu.VMEM((2, t, 1), F32), pltpu.VMEM((2, t, 1), F32),
                        pltpu.VMEM((2, t, LANES), F32)],
        compiler_params=_params(3),
        name="diff_attention",
    )(jnp.asarray(_alibi_slopes(nh)), proj, proj, proj, lam, sub_gain.reshape(1, LANES))


def _softplus(x):
    return jnp.maximum(x, 0.0) + jnp.log(1.0 + jnp.exp(-jnp.abs(x)))


def _stick_body(q_ref, k_ref, v_ref, tri_ref, o_ref, c_sc, acc_sc):
    t = D_T
    i = pl.program_id(2)
    qm = _half_masked(q_ref[...].astype(F32) * HEAD_DIM ** -0.5)
    c_sc[...] = jnp.zeros(c_sc.shape, F32)
    acc_sc[...] = jnp.zeros(acc_sc.shape, F32)
    kl = lax.broadcasted_iota(jnp.int32, (1, t), 1)
    tl = lax.broadcasted_iota(jnp.int32, (t, 1), 0)

    def step(j, diagonal):
        k0 = pl.multiple_of(j * t, t)
        k = k_ref[pl.ds(k0, t), :]
        v = v_ref[pl.ds(k0, t), :]
        tri = tri_ref[...]
        for hh in range(2):
            logit = lax.dot_general(qm[hh], k, _NT, preferred_element_type=F32)
            sp = _softplus(logit)
            log_om = -sp
            if diagonal:
                log_om = jnp.where(kl < tl, log_om, 0.0)
            between = _dot_split(log_om, tri) + c_sc[hh]
            a = jnp.exp(logit - sp + between)
            if diagonal:
                a = jnp.where(kl < tl, a, 0.0)
            acc_sc[hh] = acc_sc[hh] + jnp.dot(a.astype(BF16), v, preferred_element_type=F32)
            c_sc[hh] = c_sc[hh] + jnp.sum(log_om, axis=1, keepdims=True)

    step(i, True)

    def body(jj, carry):
        step(i - 1 - jj, False)
        return carry

    lax.fori_loop(0, i, body, 0)
    lane = lax.broadcasted_iota(jnp.int32, (1, LANES), 1)
    o_ref[...] = jnp.where(lane < HEAD_DIM, acc_sc[0], acc_sc[1]).astype(BF16)


def _stick_attention(proj, batch, seq):
    t = D_T
    nq = seq // t
    nb = D_HEADS // 2
    tri = np.tril(np.ones((t, t), np.float32), -1)
    return pl.pallas_call(
        _stick_body,
        grid=(batch, nb, nq),
        in_specs=[pl.BlockSpec((t, LANES), lambda b, h, i: (b * nq + i, h)),
                  pl.BlockSpec((seq, LANES), lambda b, h, i: (b, nb + h)),
                  pl.BlockSpec((seq, LANES), lambda b, h, i: (b, 2 * nb + h)),
                  pl.BlockSpec((t, t), lambda b, h, i: (0, 0))],
        out_specs=pl.BlockSpec((t, LANES), lambda b, h, i: (b * nq + i, h)),
        out_shape=jax.ShapeDtypeStruct((batch * seq, D_MODEL), BF16),
        scratch_shapes=[pltpu.VMEM((2, t, 1), F32), pltpu.VMEM((2, t, LANES), F32)],
        compiler_params=_params(3),
        name="stick_attention",
    )(proj, proj, proj, jnp.asarray(tri, BF16))


def _dilated_body(slopes_ref, q0_ref, q1_ref, q2_ref, k0_ref, k1_ref, k2_ref, v_ref, o_ref):
    tq = C_TQ
    hp = pl.program_id(1)
    i = pl.program_id(2)
    seq = v_ref.shape[0]
    q_start = i * tq
    t_col = q_start + lax.broadcasted_iota(jnp.int32, (tq, 1), 0)
    q_refs = (q0_ref, q1_ref, q2_ref)
    k_refs = (k0_ref, k1_ref, k2_ref)
    for hh in range(2):
        outs, lses = [], []
        for g, (w, d) in enumerate(C_PATTERNS):
            span = min(w + tq, seq)
            k_start = pl.multiple_of(jnp.maximum(q_start + tq - span, 0), tq)
            qm = _half_masked(q_refs[g][...].astype(F32) * HEAD_DIM ** -0.5)[hh]
            kk = k_refs[g][pl.ds(k_start, span), :]
            vv = v_ref[pl.ds(k_start, span), hh * C_V_DIM:(hh + 1) * C_V_DIM]
            slope = slopes_ref[g * C_HEADS + hp * 2 + hh]
            dist = t_col - (k_start + lax.broadcasted_iota(jnp.int32, (1, span), 1))
            valid = (dist >= 0) & (dist <= w) & ((dist & (d - 1)) == 0)
            s = lax.dot_general(qm, kk, _NT, preferred_element_type=F32) - slope * dist.astype(F32)
            s = jnp.where(valid, s, NEG_INF)
            m = jnp.max(s, axis=1, keepdims=True)
            e = jnp.where(valid, jnp.exp(s - m), 0.0)
            l = jnp.sum(e, axis=1, keepdims=True)
            outs.append(jnp.dot(e.astype(BF16), vv, preferred_element_type=F32) / l)
            lses.append(m + jnp.log(l))
        mx = jnp.maximum(jnp.maximum(lses[0], lses[1]), lses[2])
        ws = [jnp.exp(x - mx) for x in lses]
        tot = ws[0] + ws[1] + ws[2]
        mixed = (ws[0] * outs[0] + ws[1] * outs[1] + ws[2] * outs[2]) / tot
        o_ref[:, hh * C_V_DIM:(hh + 1) * C_V_DIM] = mixed.astype(BF16)


def _dilated_attention(proj, batch, seq):
    tq = C_TQ
    nq = seq // tq
    zb = D_MODEL // LANES
    nqk = len(C_PATTERNS) * C_HEADS // 2
    vb = (D_MODEL + 2 * nqk * LANES) // (2 * C_V_DIM)
    q_spec = lambda g: pl.BlockSpec((tq, LANES), lambda b, h, i: (b * nq + i, zb + 2 * g + h))
    k_spec = lambda g: pl.BlockSpec((seq, LANES), lambda b, h, i: (b, zb + nqk + 2 * g + h))
    return pl.pallas_call(
        _dilated_body,
        grid=(batch, 2, nq),
        in_specs=[_smem_spec(), q_spec(0), q_spec(1), q_spec(2), k_spec(0), k_spec(1), k_spec(2),
                  pl.BlockSpec((seq, 2 * C_V_DIM), lambda b, h, i: (b, vb + h))],
        out_specs=pl.BlockSpec((tq, 2 * C_V_DIM), lambda b, h, i: (b * nq + i, h)),
        out_shape=jax.ShapeDtypeStruct((batch * seq, D_MODEL), BF16),
        compiler_params=_params(3),
        name="dilated_attention",
    )(jnp.asarray(_alibi_slopes(len(C_PATTERNS) * C_HEADS)), *([proj] * 7))


def _gelu_tanh(x):
    return 0.5 * x * (1.0 + jnp.tanh(math.sqrt(2.0 / math.pi) * (x + 0.044715 * x * x * x)))


def _compress_body(ak_ref, av_ref, pos_ref, w1_ref, w2_ref, o_ref):
    n = ak_ref.shape[2]
    acc = jnp.zeros((n, LANES), F32)
    for kv, a_ref in enumerate((ak_ref, av_ref)):
        a = a_ref[0, 0].astype(F32)
        first = jnp.dot((a + pos_ref[kv, 0]).astype(BF16), w1_ref[kv, 0], preferred_element_type=F32)
        second = jnp.dot((a + pos_ref[kv, 1]).astype(BF16), w1_ref[kv, 1], preferred_element_type=F32)
        hidden = _gelu_tanh(first + pltpu.roll(second, n - 1, 0))
        acc = acc + jnp.dot(hidden.astype(BF16), w2_ref[kv], preferred_element_type=F32)
    o_ref[0, 0] = acc.astype(BF16)


def _compress(ak, av, pos, w1, w2):
    batch, hk, n, width = ak.shape
    a_spec = pl.BlockSpec((1, 1, n, width), lambda b, h: (b, h, 0, 0))
    return pl.pallas_call(
        _compress_body,
        grid=(batch, hk),
        in_specs=[a_spec, a_spec,
                  pl.BlockSpec(pos.shape, lambda b, h: (0, 0, 0, 0)),
                  pl.BlockSpec(w1.shape, lambda b, h: (0, 0, 0, 0)),
                  pl.BlockSpec(w2.shape, lambda b, h: (0, 0, 0))],
        out_specs=pl.BlockSpec((1, 1, n, LANES), lambda b, h: (b, h, 0, 0)),
        out_shape=jax.ShapeDtypeStruct((batch, hk, n, LANES), BF16),
        compiler_params=_params(2),
        name="nsa_compress",
    )(ak, av, pos, w1, w2)


def _nsa_body(slopes_ref, q_ref, sel_ref, win_ref, cmp_ref, ov_ref, oc_ref, os_ref, ow_ref,
              m_sc, l_sc, acc_sc):
    tq, tk, g_n = A_TQ, A_TK, A_GROUP
    hk = pl.program_id(1)
    i = pl.program_id(2)
    q_start = i * tq
    lane = lax.broadcasted_iota(jnp.int32, (1, LANES), 1)
    t_col = q_start + lax.broadcasted_iota(jnp.int32, (tq, 1), 0)
    slopes = [slopes_ref[hk * g_n + g] for g in range(g_n)]

    q32 = q_ref[...].astype(F32) * HEAD_DIM ** -0.5
    qpad = []
    for g in range(g_n):
        blk = q32[:, (g // 2) * LANES:(g // 2 + 1) * LANES]
        if g % 2:
            blk = pltpu.roll(blk, HEAD_DIM, 1)
        qpad.append(jnp.where(lane < HEAD_DIM, blk, 0.0).astype(BF16))

    def softmax_attend(g, kv, dist, valid):
        s = lax.dot_general(qpad[g], kv, _NT, preferred_element_type=F32) - slopes[g] * dist.astype(F32)
        s = jnp.where(valid, s, NEG_INF)
        e = jnp.where(valid, jnp.exp(s - jnp.max(s, axis=1, keepdims=True)), 0.0)
        p = e / jnp.maximum(jnp.sum(e, axis=1, keepdims=True), TINY)
        return p, jnp.dot(p.astype(BF16), kv, preferred_element_type=F32)

    def to_token_major(per_head):
        blocks = [jnp.where(lane < HEAD_DIM, pltpu.roll(per_head[2 * a], HEAD_DIM, 1), per_head[2 * a + 1])
                  for a in range(g_n // 2)]
        return jnp.concatenate(blocks, axis=1).astype(BF16)

    n_cmp = cmp_ref.shape[2]
    ckv = cmp_ref[0, 0]
    cmp_end = A_CMP_STRIDE * lax.broadcasted_iota(jnp.int32, (1, n_cmp), 1) + (A_CMP_BLOCK - 1)
    cdist = t_col - cmp_end
    cvalid = cdist >= 0
    outs = []
    p_sum = jnp.zeros((tq, n_cmp), F32)
    for g in range(g_n):
        p, o = softmax_attend(g, ckv, cdist, cvalid)
        outs.append(o)
        p_sum = p_sum + p
    oc_ref[...] = to_token_major(outs)

    n_sel = ov_ref.shape[1]
    imp = _dot_split(p_sum, ov_ref[...])
    jl = lax.broadcasted_iota(jnp.int32, (tq, n_sel), 1)
    cur = jnp.right_shift(t_col, A_SEL_SHIFT)
    forced = (jl == 0) | (jl == cur) | (jl == cur - 1)
    imp = jnp.where(jl > cur, -1.0, imp + jnp.where(forced, A_FORCE_BONUS, 0.0))
    jlf = jl.astype(F32)
    selected = jnp.zeros((tq, n_sel), F32)
    for _ in range(min(A_SEL_TOPK, n_sel)):
        top = jnp.max(imp, axis=1, keepdims=True)
        first = jnp.min(jnp.where(imp == top, jlf, float(n_sel)), axis=1, keepdims=True)
        pick = jlf == first
        selected = jnp.where(pick, 1.0, selected)
        imp = jnp.where(pick, -2.0, imp)
    selected = selected.astype(BF16)

    m_sc[...] = jnp.full(m_sc.shape, NEG_INF, F32)
    l_sc[...] = jnp.zeros(l_sc.shape, F32)
    acc_sc[...] = jnp.zeros(acc_sc.shape, F32)
    blk_row = lax.broadcasted_iota(jnp.int32, (n_sel, tk), 0)
    key_col = lax.broadcasted_iota(jnp.int32, (n_sel, tk), 1)
    kl = lax.broadcasted_iota(jnp.int32, (1, tk), 1)

    def sel_step(kt, carry):
        k_start = pl.multiple_of(kt * tk, tk)
        kv = sel_ref[pl.ds(k_start, tk), :]
        expand = jnp.where(jnp.right_shift(k_start + key_col, A_SEL_SHIFT) == blk_row, 1.0, 0.0).astype(BF16)
        in_sel = jnp.dot(selected, expand, preferred_element_type=F32)
        dist = t_col - (k_start + kl)
        valid = (in_sel > 0.5) & (dist >= 0)
        bias = dist.astype(F32)
        for g in range(g_n):
            s = lax.dot_general(qpad[g], kv, _NT, preferred_element_type=F32) - slopes[g] * bias
            s = jnp.where(valid, s, NEG_INF)
            m_old = m_sc[g]
            m_new = jnp.maximum(m_old, jnp.max(s, axis=1, keepdims=True))
            p = jnp.where(valid, jnp.exp(s - m_new), 0.0)
            alpha = jnp.exp(m_old - m_new)
            l_sc[g] = alpha * l_sc[g] + jnp.sum(p, axis=1, keepdims=True)
            acc_sc[g] = alpha * acc_sc[g] + jnp.dot(p.astype(BF16), kv, preferred_element_type=F32)
            m_sc[g] = m_new
        return carry

    lax.fori_loop(0, (q_start + tq - 1) // tk + 1, sel_step, 0)
    os_ref[...] = to_token_major([acc_sc[g] / jnp.maximum(l_sc[g], TINY) for g in range(g_n)])

    seq = win_ref.shape[0]
    span = min(A_WINDOW + tq, seq)
    w_start = pl.multiple_of(jnp.maximum(q_start + tq - span, 0), tq)
    wkv = win_ref[pl.ds(w_start, span), :]
    wdist = t_col - (w_start + lax.broadcasted_iota(jnp.int32, (1, span), 1))
    wvalid = (wdist >= 0) & (wdist < A_WINDOW)
    ow_ref[...] = to_token_major([softmax_attend(g, wkv, wdist, wvalid)[1] for g in range(g_n)])


def _nsa_attention(proj, cmp_kv, batch, seq):
    tq = A_TQ
    nq = seq // tq
    hk_n = A_KV_HEADS
    n_cmp = cmp_kv.shape[2]
    n_sel = seq // A_SEL_BLOCK
    ov = np.zeros((n_cmp, n_sel), np.float32)
    real = (seq - A_CMP_BLOCK) // A_CMP_STRIDE + 1
    cidx = A_CMP_STRIDE * np.arange(real)[:, None] + np.arange(A_CMP_BLOCK)[None, :]
    np.add.at(ov, (np.repeat(np.arange(real), A_CMP_BLOCK), (cidx // A_SEL_BLOCK).ravel()), 1.0 / A_CMP_BLOCK)
    qb = D_MODEL // (A_GROUP * HEAD_DIM)
    sb = 2 * D_MODEL // LANES
    wb = sb + hk_n
    o_spec = pl.BlockSpec((tq, A_GROUP * HEAD_DIM), lambda b, h, i: (b * nq + i, h))
    o_shape = jax.ShapeDtypeStruct((batch * seq, D_MODEL), BF16)
    return pl.pallas_call(
        _nsa_body,
        grid=(batch, hk_n, nq),
        in_specs=[_smem_spec(),
                  pl.BlockSpec((tq, A_GROUP * HEAD_DIM), lambda b, h, i: (b * nq + i, qb + h)),
                  pl.BlockSpec((seq, LANES), lambda b, h, i: (b, sb + h)),
                  pl.BlockSpec((seq, LANES), lambda b, h, i: (b, wb + h)),
                  pl.BlockSpec((1, 1, n_cmp, LANES), lambda b, h, i: (b, h, 0, 0)),
                  pl.BlockSpec((n_cmp, n_sel), lambda b, h, i: (0, 0))],
        out_specs=[o_spec, o_spec, o_spec],
        out_shape=[o_shape, o_shape, o_shape],
        scratch_shapes=[pltpu.VMEM((A_GROUP, tq, 1), F32), pltpu.VMEM((A_GROUP, tq, 1), F32),
                        pltpu.VMEM((A_GROUP, tq, LANES), F32)],
        compiler_params=_params(3),
        name="nsa_attention",
    )(jnp.asarray(_alibi_slopes(A_HEADS)), proj, proj, proj, cmp_kv, jnp.asarray(ov, BF16))


def _nsa_layer(h, batch, seq, gain_pre, gain_post, w_in, w_out, pos_k, pos_v, w1_k, w2_k, w1_v, w2_v):
    dm, dh, hk_n = D_MODEL, HEAD_DIM, A_KV_HEADS
    kvw = hk_n * dh
    q0, kc0, vc0, ks0, vs0, kw0, vw0, gl0, z0 = np.cumsum((0, dm) + (kvw,) * 6 + (3 * A_HEADS,))
    cols = [np.arange(z0, z0 + dm), np.arange(q0, q0 + dm)]
    for k0, v0 in ((ks0, vs0), (kw0, vw0)):
        for hh in range(hk_n):
            cols += [np.arange(k0 + hh * dh, k0 + (hh + 1) * dh), np.arange(v0 + hh * dh, v0 + (hh + 1) * dh)]
    cols += [np.arange(kc0, kc0 + kvw), np.arange(vc0, vc0 + kvw), np.arange(gl0, gl0 + 3 * A_HEADS)]
    cols = np.concatenate(cols)
    w = jnp.pad(w_in[:, cols], ((0, 0), (0, LANES - 3 * A_HEADS))).astype(BF16)
    proj = _proj_in(h, gain_pre, w)

    grp = A_CMP_STRIDE
    cmp_col = 2 * dm + 4 * kvw

    def groups(c0):
        a = proj[:, c0:c0 + kvw].reshape(batch, seq // grp, grp, hk_n, dh)
        return a.transpose(0, 3, 1, 2, 4).reshape(batch, hk_n, seq // grp, grp * dh)

    half = grp * dh
    pos = jnp.stack([pos_k.reshape(2, 1, half), pos_v.reshape(2, 1, half)])
    w1 = jnp.stack([w1_k.reshape(2, half, A_CMP_HIDDEN), w1_v.reshape(2, half, A_CMP_HIDDEN)]).astype(BF16)
    w2 = jnp.stack([jnp.pad(w2_k, ((0, 0), (0, dh))), jnp.pad(w2_v, ((0, 0), (dh, 0)))]).astype(BF16)
    cmp_kv = _compress(groups(cmp_col), groups(cmp_col + kvw), pos, w1, w2)

    oc, osel, ow = _nsa_attention(proj, cmp_kv, batch, seq)
    gl_blk = (cmp_col + 2 * kvw) // LANES
    return _proj_out([oc, osel, ow], proj, 0, w_out.astype(BF16), gain_post, h, gl_blk=gl_blk)


def kernel(x, norm_pre, norm_post, a_w_in, a_w_out, a_cmp_pos_k, a_cmp_pos_v, a_cmp_w1_k, a_cmp_w2_k,
           a_cmp_w1_v, a_cmp_w2_v, b_w_in, b_w_out, b_lambda, b_sub_gain, c_w_in, c_w_out, d_w_in, d_w_out):
    batch, seq, dm = x.shape
    h = x.reshape(batch * seq, dm)
    for i in range(DEPTH):
        mixer, j = i % 4, i // 4
        if mixer == 0:
            h = _nsa_layer(h, batch, seq, norm_pre[i], norm_post[i], a_w_in[j], a_w_out[j],
                           a_cmp_pos_k[j], a_cmp_pos_v[j], a_cmp_w1_k[j], a_cmp_w2_k[j],
                           a_cmp_w1_v[j], a_cmp_w2_v[j])
            continue
        if mixer == 1:
            proj = _proj_in(h, norm_pre[i], b_w_in[j].astype(BF16))
            lambda_init = 0.8 - 0.6 * math.exp(-0.3 * i)
            o = _diff_attention(proj, b_lambda[j], b_sub_gain[j], lambda_init, batch, seq)
            z_blk, w_out = 3, b_w_out[j]
        elif mixer == 2:
            qk = 2 * len(C_PATTERNS) * C_HEADS * HEAD_DIM
            w = jnp.concatenate([c_w_in[j][:, qk + dm:], c_w_in[j][:, :qk + dm]], axis=1)
            proj = _proj_in(h, norm_pre[i], w.astype(BF16))
            o = _dilated_attention(proj, batch, seq)
            z_blk, w_out = 0, c_w_out[j]
        else:
            proj = _proj_in(h, norm_pre[i], d_w_in[j].astype(BF16))
            o = _stick_attention(proj, batch, seq)
            z_blk, w_out = 3, d_w_out[j]
        h = _proj_out([o], proj, z_blk, w_out.astype(BF16), norm_post[i], h)
    return h.reshape(batch, seq, dm)
```

```python
import functools
import math

import jax
import jax.numpy as jnp
import numpy as np
from jax import lax
from jax.experimental import pallas as pl
from jax.experimental.pallas import tpu as pltpu

F32 = jnp.float32
BF16 = jnp.bfloat16

D_MODEL = 1024
HEAD_DIM = 64
DEPTH = 4
RMS_EPS = 1e-6
NEG_INF = -1e30
TINY = 1e-30
LANES = 128
ONES_ROWS = 16
LOG2E = math.log2(math.e)
VMEM_LIMIT = 48 * 1024 * 1024

A_HEADS = 16
A_KV_HEADS = 4
A_GROUP = 4
A_CMP_BLOCK = 32
A_CMP_STRIDE = 16
A_CMP_HIDDEN = 256
A_SEL_BLOCK = 64
A_SEL_SHIFT = 6
A_SEL_TOPK = 16
A_WINDOW = 512
A_FORCE_BONUS = 1e3
A_T = 256
B_HEADS = 8
B_T = 256
C_PATTERNS = ((128, 1), (512, 4), (2048, 16))
C_HEADS = 4
C_V_DIM = 256
C_TQ = 128
D_HEADS = 16
D_T = 256

PROJ_TM = 512
PROJ_CHUNK = 512

_NT = (((1,), (1,)), ((), ()))


def _alibi_slopes(n):
    return np.array([2.0 ** (-8.0 * (i + 1) / n) for i in range(n)], np.float32)


def _params(n_grid):
    return pltpu.CompilerParams(dimension_semantics=("arbitrary",) * n_grid,
                                vmem_limit_bytes=VMEM_LIMIT)


def _split_bf16(x):
    hi = x.astype(BF16)
    lo = (x - hi.astype(F32)).astype(BF16)
    return hi, lo


def _sigmoid(x):
    return 1.0 / (1.0 + jnp.exp(-x))


def _proj_in_body(h_ref, g_ref, w_ref, o_ref):
    x = h_ref[...]
    ms = jnp.mean(x * x, axis=-1, keepdims=True)
    u = (x * lax.rsqrt(ms + RMS_EPS) * g_ref[...]).astype(BF16)
    n = o_ref.shape[1]
    for c in range(0, n, PROJ_CHUNK):
        e = min(c + PROJ_CHUNK, n)
        o_ref[:, c:e] = jnp.dot(u, w_ref[:, c:e], preferred_element_type=F32).astype(BF16)


def _proj_in(h, gain, w):
    m, n = h.shape[0], w.shape[1]
    return pl.pallas_call(
        _proj_in_body,
        grid=(m // PROJ_TM,),
        in_specs=[pl.BlockSpec((PROJ_TM, D_MODEL), lambda i: (i, 0)),
                  pl.BlockSpec((1, D_MODEL), lambda i: (0, 0)),
                  pl.BlockSpec((D_MODEL, n), lambda i: (0, 0))],
        out_specs=pl.BlockSpec((PROJ_TM, n), lambda i: (i, 0)),
        out_shape=jax.ShapeDtypeStruct((m, n), BF16),
        compiler_params=_params(1),
        name="proj_in",
    )(h, gain.reshape(1, D_MODEL), w)


def _finish(o, z_ref, w_ref, g_ref, h_ref, out_ref):
    z = z_ref[...].astype(F32)
    gated = (o * (z * _sigmoid(z))).astype(BF16)
    y = jnp.dot(gated, w_ref[...], preferred_element_type=F32)
    ms = jnp.mean(y * y, axis=-1, keepdims=True)
    out_ref[...] = h_ref[...] + y * lax.rsqrt(ms + RMS_EPS) * g_ref[...]


def _proj_out_body(o_ref, z_ref, w_ref, g_ref, h_ref, out_ref):
    _finish(o_ref[...].astype(F32), z_ref, w_ref, g_ref, h_ref, out_ref)


def _proj_out_nsa_body(oc_ref, os_ref, ow_ref, gl_ref, eg_ref, z_ref, w_ref, g_ref, h_ref, out_ref):
    hi, lo = _split_bf16(_sigmoid(gl_ref[...].astype(F32)))
    eg = eg_ref[...]
    gates = jnp.dot(hi, eg, preferred_element_type=F32) + jnp.dot(lo, eg, preferred_element_type=F32)
    o = (gates[:, :D_MODEL] * oc_ref[...].astype(F32)
         + gates[:, D_MODEL:2 * D_MODEL] * os_ref[...].astype(F32)
         + gates[:, 2 * D_MODEL:] * ow_ref[...].astype(F32))
    _finish(o, z_ref, w_ref, g_ref, h_ref, out_ref)


def _proj_out(o_list, proj, z_blk, w_out, gain, h, gl_blk=None):
    m = h.shape[0]
    tm = PROJ_TM
    row = lambda i: (i, 0)
    fixed = lambda i: (0, 0)
    o_specs = [pl.BlockSpec((tm, D_MODEL), row) for _ in o_list]
    tail_specs = [pl.BlockSpec((tm, D_MODEL), lambda i: (i, z_blk)),
                  pl.BlockSpec((D_MODEL, D_MODEL), fixed),
                  pl.BlockSpec((1, D_MODEL), fixed),
                  pl.BlockSpec((tm, D_MODEL), row)]
    tail = [proj, w_out, gain.reshape(1, D_MODEL), h]
    if gl_blk is None:
        body, in_specs, args = _proj_out_body, o_specs + tail_specs, list(o_list) + tail
    else:
        eg = np.zeros((LANES, 3 * D_MODEL), np.float32)
        for br in range(3):
            for hd in range(A_HEADS):
                c0 = br * D_MODEL + hd * HEAD_DIM
                eg[br * A_HEADS + hd, c0:c0 + HEAD_DIM] = 1.0
        body = _proj_out_nsa_body
        in_specs = o_specs + [pl.BlockSpec((tm, LANES), lambda i: (i, gl_blk)),
                              pl.BlockSpec((LANES, 3 * D_MODEL), fixed)] + tail_specs
        args = list(o_list) + [proj, jnp.asarray(eg, BF16)] + tail
    return pl.pallas_call(
        body,
        grid=(m // tm,),
        in_specs=in_specs,
        out_specs=pl.BlockSpec((tm, D_MODEL), row),
        out_shape=jax.ShapeDtypeStruct((m, D_MODEL), F32),
        compiler_params=_params(1),
        name="proj_out",
    )(*args)


def _half_masked(q32):
    lane = lax.broadcasted_iota(jnp.int32, (1, LANES), 1)
    lo = jnp.where(lane < HEAD_DIM, q32, 0.0).astype(BF16)
    hi = jnp.where(lane >= HEAD_DIM, q32, 0.0).astype(BF16)
    return lo, hi


def _smem_spec():
    return pl.BlockSpec(memory_space=pltpu.SMEM)


def _tile_transposed(cols, batch, n_heads, width, t):
    seq = cols.shape[0] // batch
    return cols.reshape(batch, seq // t, t, n_heads, width).transpose(0, 3, 1, 4, 2)


def _row_iota(shape):
    return lax.broadcasted_iota(jnp.int32, shape, 0)


def _lane_iota(shape):
    return lax.broadcasted_iota(jnp.int32, shape, 1)


def _pipeline3(i, score, softmax, accumulate):
    score(0, 0)
    score(jnp.minimum(1, i), 1)
    softmax(0, 0, True)

    def pair(r, carry):
        s = 2 * r + 1
        accumulate(s - 1, 0)
        score(s + 1, 0)
        softmax(s, 1, False)
        accumulate(s, 1)
        score(jnp.minimum(s + 2, i), 1)
        softmax(s + 1, 0, False)
        return carry

    lax.fori_loop(0, i // 2, pair, 0)

    @pl.when(i % 2 == 1)
    def _():
        accumulate(i - 1, 0)
        softmax(i, 1, False)
        accumulate(i, 1)

    @pl.when(i % 2 == 0)
    def _():
        accumulate(i, 0)


def _diff_body(slopes_ref, q_ref, k_ref, vt_ref, lam_ref, sg_ref, o_ref,
               m_sc, acc_sc, b0_sc, u_sc, p_sc, a_sc, *, lambda_init):
    t = B_T
    dv = 2 * HEAD_DIM
    hd = pl.program_id(1)
    i = pl.program_id(2)
    slope2 = slopes_ref[hd] * LOG2E

    @pl.when(i == 0)
    def _():
        b0_sc[...] = slope2 * _row_iota((t, t)).astype(F32)

    qm = _half_masked(q_ref[...].astype(F32) * (HEAD_DIM ** -0.5 * LOG2E))
    m_sc[...] = jnp.full(m_sc.shape, NEG_INF, F32)
    acc_sc[...] = jnp.zeros(acc_sc.shape, F32)
    ones = jnp.ones((ONES_ROWS, t), BF16)

    def score(s, par):
        k = k_ref[pl.ds(pl.multiple_of((i - s) * t, t), t), :]
        for mp in range(2):
            u_sc[par, mp] = lax.dot_general(k, qm[mp], _NT, preferred_element_type=F32)

    def softmax(s, par, diagonal):
        cj = slope2 * jnp.asarray(-s * t, jnp.int32).astype(F32)
        for mp in range(2):
            u = u_sc[par, mp] + b0_sc[...]
            if diagonal:
                u = jnp.where(_row_iota((t, t)) <= _lane_iota((t, t)), u, NEG_INF)
            m_old = m_sc[mp]
            m_new = jnp.maximum(m_old, jnp.max(u, axis=0, keepdims=True) + cj)
            p_sc[par, mp] = jnp.exp2(u - (m_new - cj)).astype(BF16)
            a_sc[par, mp] = jnp.exp2(m_old - m_new)
            m_sc[mp] = m_new

    def accumulate(s, par):
        vaug = jnp.concatenate([vt_ref[0, 0, i - s], ones], axis=0)
        pv = [jnp.dot(vaug, p_sc[par, mp], preferred_element_type=F32) for mp in range(2)]
        for mp in range(2):
            acc_sc[mp] = a_sc[par, mp] * acc_sc[mp] + pv[mp]

    _pipeline3(i, score, softmax, accumulate)

    lam = lam_ref[...]
    lam_full = (jnp.exp(jnp.sum(lam[0:1] * lam[1:2], axis=1, keepdims=True))
                - jnp.exp(jnp.sum(lam[2:3] * lam[3:4], axis=1, keepdims=True)) + lambda_init)
    o0 = acc_sc[0, :dv] / jnp.maximum(acc_sc[0, dv:dv + 1], TINY)
    o1 = acc_sc[1, :dv] / jnp.maximum(acc_sc[1, dv:dv + 1], TINY)
    a = o0 - lam_full * o1
    ms = jnp.mean(a * a, axis=0, keepdims=True)
    y = (a * lax.rsqrt(ms + RMS_EPS) * sg_ref[...]) * (1.0 - lambda_init)
    o_ref[...] = y.T.astype(BF16)


def _diff_attention(proj, lam, sub_gain, lambda_init, batch, seq):
    t = B_T
    nq = seq // t
    nh = B_HEADS
    dv = 2 * HEAD_DIM
    vt = _tile_transposed(proj[:, 2 * D_MODEL:3 * D_MODEL], batch, nh, dv, t)
    return pl.pallas_call(
        functools.partial(_diff_body, lambda_init=lambda_init),
        grid=(batch, nh, nq),
        in_specs=[_smem_spec(),
                  pl.BlockSpec((t, LANES), lambda b, h, i: (b * nq + i, h)),
                  pl.BlockSpec((seq, LANES), lambda b, h, i: (b, nh + h)),
                  pl.BlockSpec((1, 1, nq, dv, t), lambda b, h, i: (b, h, 0, 0, 0)),
                  pl.BlockSpec((4, HEAD_DIM), lambda b, h, i: (0, 0)),
                  pl.BlockSpec((dv, 1), lambda b, h, i: (0, 0))],
        out_specs=pl.BlockSpec((t, LANES), lambda b, h, i: (b * nq + i, h)),
        out_shape=jax.ShapeDtypeStruct((batch * seq, D_MODEL), BF16),
        scratch_shapes=[pltpu.VMEM((2, 1, t), F32), pltpu.VMEM((2, dv + ONES_ROWS, t), F32),
                        pltpu.VMEM((t, t), F32),
                        pltpu.VMEM((2, 2, t, t), F32), pltpu.VMEM((2, 2, t, t), BF16),
                        pltpu.VMEM((2, 2, 1, t), F32)],
        compiler_params=_params(3),
        name="diff_attention",
    )(jnp.asarray(_alibi_slopes(nh)), proj, proj, vt, lam, sub_gain.reshape(dv, 1))


def _softplus(x):
    return jnp.maximum(x, 0.0) + jnp.log(1.0 + jnp.exp(-jnp.abs(x)))


def _stick_body(q_ref, k_ref, vt_ref, tri_ref, o_ref, c_sc, acc_sc, u_sc, ls_sc, hl_sc, a_sc):
    t = D_T
    dh = HEAD_DIM
    i = pl.program_id(2)
    qm = _half_masked(q_ref[...].astype(F32) * dh ** -0.5)
    c_sc[...] = jnp.zeros(c_sc.shape, F32)
    acc_sc[...] = jnp.zeros(acc_sc.shape, F32)
    a_sc[...] = jnp.zeros(a_sc.shape, BF16)

    def logits(s, par):
        k = k_ref[pl.ds(pl.multiple_of((i - s) * t, t), t), :]
        for hh in range(2):
            u_sc[par, hh] = lax.dot_general(k, qm[hh], _NT, preferred_element_type=F32)

    def softplus(par, diagonal):
        for hh in range(2):
            logit = u_sc[par, hh]
            sp = _softplus(logit)
            log_sig = logit - sp
            if diagonal:
                before = _row_iota((t, t)) < _lane_iota((t, t))
                sp = jnp.where(before, sp, 0.0)
                log_sig = jnp.where(before, log_sig, NEG_INF)
            ls_sc[par, hh] = log_sig
            hi, lo = _split_bf16(sp)
            hl_sc[par, hh, 0] = hi
            hl_sc[par, hh, 1] = lo

    def suffix_sums(par):
        tri = tri_ref[...]
        return [jnp.dot(tri, hl_sc[par, hh, 0], preferred_element_type=F32)
                + jnp.dot(tri, hl_sc[par, hh, 1], preferred_element_type=F32) for hh in range(2)]

    def weights(sums, par):
        for hh in range(2):
            a_sc[par, hh] = jnp.exp(ls_sc[par, hh] + (sums[hh][:t] + c_sc[hh])).astype(BF16)
            c_sc[hh] = c_sc[hh] + sums[hh][t:t + 1]

    def values(s, par):
        vt = vt_ref[0, 0, jnp.minimum(i - s, i)]
        for hh in range(2):
            acc_sc[hh] = acc_sc[hh] + jnp.dot(vt[hh * dh:(hh + 1) * dh], a_sc[par, hh],
                                              preferred_element_type=F32)

    def iteration(s, par, prefetch):
        sums = suffix_sums(1 - par)
        values(s - 2, par)
        if prefetch:
            logits(jnp.minimum(s + 1, i), 1 - par)
        softplus(par, False)
        weights(sums, 1 - par)

    def drain(par):
        sums = suffix_sums(par)
        values(i - 1, 1 - par)
        weights(sums, par)
        values(i, par)

    logits(0, 0)
    logits(jnp.minimum(1, i), 1)
    softplus(0, True)

    def pair(r, carry):
        s = 2 * r + 1
        iteration(s, 1, True)
        iteration(s + 1, 0, True)
        return carry

    lax.fori_loop(0, i // 2, pair, 0)

    @pl.when(i % 2 == 1)
    def _():
        iteration(i, 1, False)
        drain(1)

    @pl.when(i % 2 == 0)
    def _():
        drain(0)

    o_ref[...] = jnp.concatenate([acc_sc[0], acc_sc[1]], axis=0).T.astype(BF16)


def _stick_attention(proj, batch, seq):
    t = D_T
    nq = seq // t
    nb = D_HEADS // 2
    tri = -np.concatenate([np.triu(np.ones((t, t), np.float32), 1),
                           np.ones((ONES_ROWS, t), np.float32)])
    vt = _tile_transposed(proj[:, 2 * D_MODEL:3 * D_MODEL], batch, nb, LANES, t)
    return pl.pallas_call(
        _stick_body,
        grid=(batch, nb, nq),
        in_specs=[pl.BlockSpec((t, LANES), lambda b, h, i: (b * nq + i, h)),
                  pl.BlockSpec((seq, LANES), lambda b, h, i: (b, nb + h)),
                  pl.BlockSpec((1, 1, nq, LANES, t), lambda b, h, i: (b, h, 0, 0, 0)),
                  pl.BlockSpec((t + ONES_ROWS, t), lambda b, h, i: (0, 0))],
        out_specs=pl.BlockSpec((t, LANES), lambda b, h, i: (b * nq + i, h)),
        out_shape=jax.ShapeDtypeStruct((batch * seq, D_MODEL), BF16),
        scratch_shapes=[pltpu.VMEM((2, 1, t), F32), pltpu.VMEM((2, HEAD_DIM, t), F32),
                        pltpu.VMEM((2, 2, t, t), F32), pltpu.VMEM((2, 2, t, t), F32),
                        pltpu.VMEM((2, 2, 2, t, t), BF16), pltpu.VMEM((2, 2, t, t), BF16)],
        compiler_params=_params(3),
        name="stick_attention",
    )(proj, proj, vt, jnp.asarray(tri, BF16))


def _dilated_body(slopes_ref, q0_ref, q1_ref, q2_ref, k0_ref, k1_ref, k2_ref, v_ref, o_ref):
    tq = C_TQ
    hp = pl.program_id(1)
    i = pl.program_id(2)
    seq = v_ref.shape[0]
    q_start = i * tq
    t_col = q_start + lax.broadcasted_iota(jnp.int32, (tq, 1), 0)
    q_refs = (q0_ref, q1_ref, q2_ref)
    k_refs = (k0_ref, k1_ref, k2_ref)
    for hh in range(2):
        outs, lses = [], []
        for g, (w, d) in enumerate(C_PATTERNS):
            span = min(w + tq, seq)
            k_start = pl.multiple_of(jnp.maximum(q_start + tq - span, 0), tq)
            qm = _half_masked(q_refs[g][...].astype(F32) * HEAD_DIM ** -0.5)[hh]
            kk = k_refs[g][pl.ds(k_start, span), :]
            vv = v_ref[pl.ds(k_start, span), hh * C_V_DIM:(hh + 1) * C_V_DIM]
            slope = slopes_ref[g * C_HEADS + hp * 2 + hh]
            dist = t_col - (k_start + lax.broadcasted_iota(jnp.int32, (1, span), 1))
            valid = (dist >= 0) & (dist <= w) & ((dist & (d - 1)) == 0)
            s = lax.dot_general(qm, kk, _NT, preferred_element_type=F32) - slope * dist.astype(F32)
            s = jnp.where(valid, s, NEG_INF)
            m = jnp.max(s, axis=1, keepdims=True)
            e = jnp.where(valid, jnp.exp(s - m), 0.0)
            l = jnp.sum(e, axis=1, keepdims=True)
            outs.append(jnp.dot(e.astype(BF16), vv, preferred_element_type=F32) / l)
            lses.append(m + jnp.log(l))
        mx = jnp.maximum(jnp.maximum(lses[0], lses[1]), lses[2])
        ws = [jnp.exp(x - mx) for x in lses]
        tot = ws[0] + ws[1] + ws[2]
        mixed = (ws[0] * outs[0] + ws[1] * outs[1] + ws[2] * outs[2]) / tot
        o_ref[:, hh * C_V_DIM:(hh + 1) * C_V_DIM] = mixed.astype(BF16)


def _dilated_attention(proj, batch, seq):
    tq = C_TQ
    nq = seq // tq
    zb = D_MODEL // LANES
    nqk = len(C_PATTERNS) * C_HEADS // 2
    vb = (D_MODEL + 2 * nqk * LANES) // (2 * C_V_DIM)
    q_spec = lambda g: pl.BlockSpec((tq, LANES), lambda b, h, i: (b * nq + i, zb + 2 * g + h))
    k_spec = lambda g: pl.BlockSpec((seq, LANES), lambda b, h, i: (b, zb + nqk + 2 * g + h))
    return pl.pallas_call(
        _dilated_body,
        grid=(batch, 2, nq),
        in_specs=[_smem_spec(), q_spec(0), q_spec(1), q_spec(2), k_spec(0), k_spec(1), k_spec(2),
                  pl.BlockSpec((seq, 2 * C_V_DIM), lambda b, h, i: (b, vb + h))],
        out_specs=pl.BlockSpec((tq, 2 * C_V_DIM), lambda b, h, i: (b * nq + i, h)),
        out_shape=jax.ShapeDtypeStruct((batch * seq, D_MODEL), BF16),
        compiler_params=_params(3),
        name="dilated_attention",
    )(jnp.asarray(_alibi_slopes(len(C_PATTERNS) * C_HEADS)), *([proj] * 7))


def _gelu_tanh(x):
    return 0.5 * x * (1.0 + jnp.tanh(math.sqrt(2.0 / math.pi) * (x + 0.044715 * x * x * x)))


def _compress_body(ak_ref, av_ref, pos_ref, w1_ref, w2_ref, o_ref):
    n = ak_ref.shape[2]
    acc = jnp.zeros((n, LANES), F32)
    for kv, a_ref in enumerate((ak_ref, av_ref)):
        a = a_ref[0, 0].astype(F32)
        first = jnp.dot((a + pos_ref[kv, 0]).astype(BF16), w1_ref[kv, 0], preferred_element_type=F32)
        second = jnp.dot((a + pos_ref[kv, 1]).astype(BF16), w1_ref[kv, 1], preferred_element_type=F32)
        hidden = _gelu_tanh(first + pltpu.roll(second, n - 1, 0))
        acc = acc + jnp.dot(hidden.astype(BF16), w2_ref[kv], preferred_element_type=F32)
    o_ref[0, 0] = acc.astype(BF16)


def _compress(ak, av, pos, w1, w2):
    batch, hk, n, width = ak.shape
    a_spec = pl.BlockSpec((1, 1, n, width), lambda b, h: (b, h, 0, 0))
    return pl.pallas_call(
        _compress_body,
        grid=(batch, hk),
        in_specs=[a_spec, a_spec,
                  pl.BlockSpec(pos.shape, lambda b, h: (0, 0, 0, 0)),
                  pl.BlockSpec(w1.shape, lambda b, h: (0, 0, 0, 0)),
                  pl.BlockSpec(w2.shape, lambda b, h: (0, 0, 0))],
        out_specs=pl.BlockSpec((1, 1, n, LANES), lambda b, h: (b, h, 0, 0)),
        out_shape=jax.ShapeDtypeStruct((batch, hk, n, LANES), BF16),
        compiler_params=_params(2),
        name="nsa_compress",
    )(ak, av, pos, w1, w2)


def _nsa_body(slopes_ref, q_ref, sel_ref, vst_ref, win_ref, vwt_ref, cmp_ref, vct_ref, ovt_ref,
              oc_ref, os_ref, ow_ref, m_sc, acc_sc, b0_sc, selt_sc, u_sc, p_sc, a_sc):
    t, g_n, dh = A_T, A_GROUP, HEAD_DIM
    hk = pl.program_id(1)
    i = pl.program_id(2)
    q_start = i * t
    slopes2 = [slopes_ref[hk * g_n + g] * LOG2E for g in range(g_n)]

    @pl.when(i == 0)
    def _():
        rows = _row_iota((t, t)).astype(F32)
        for g in range(g_n):
            b0_sc[g] = slopes2[g] * rows

    lane = _lane_iota((1, LANES))
    t_lane = q_start + _lane_iota((1, t))

    q32 = q_ref[...].astype(F32) * (dh ** -0.5 * LOG2E)
    qpad = []
    for g in range(g_n):
        blk = q32[:, (g // 2) * LANES:(g // 2 + 1) * LANES]
        if g % 2:
            blk = pltpu.roll(blk, dh, 1)
        qpad.append(jnp.where(lane < dh, blk, 0.0).astype(BF16))

    def store_token_major(ref, per_head):
        ref[...] = jnp.concatenate(per_head, axis=0).T.astype(BF16)

    n_cmp = cmp_ref.shape[2]
    ckv = cmp_ref[0, 0]
    vct = vct_ref[0, 0]
    cmp_end = A_CMP_STRIDE * _row_iota((n_cmp, t)) + (A_CMP_BLOCK - 1)
    cvalid = t_lane >= cmp_end
    cpos = (cmp_end - q_start).astype(F32)
    raw = [lax.dot_general(ckv, qpad[g], _NT, preferred_element_type=F32) for g in range(g_n)]
    probs = []
    for g in range(g_n):
        u = jnp.where(cvalid, raw[g] + slopes2[g] * cpos, NEG_INF)
        e = jnp.where(cvalid, jnp.exp2(u - jnp.max(u, axis=0, keepdims=True)), 0.0)
        probs.append(e * (1.0 / jnp.maximum(jnp.sum(e, axis=0, keepdims=True), TINY)))
    store_token_major(oc_ref, [jnp.dot(vct, p.astype(BF16), preferred_element_type=F32) for p in probs])
    p_sum = (probs[0] + probs[1]) + (probs[2] + probs[3])

    n_sel = ovt_ref.shape[0]
    hi, lo = _split_bf16(p_sum)
    ovt = ovt_ref[...]
    imp = jnp.dot(ovt, hi, preferred_element_type=F32) + jnp.dot(ovt, lo, preferred_element_type=F32)
    jrow = _row_iota((n_sel, t))
    cur = jnp.right_shift(t_lane, A_SEL_SHIFT)
    forced = (jrow == 0) | (jrow == cur) | (jrow == cur - 1)
    imp = jnp.where(jrow > cur, -1.0, imp + jnp.where(forced, A_FORCE_BONUS, 0.0))
    jrow_f = jrow.astype(F32)
    selected = jnp.zeros((n_sel, t), F32)
    for _ in range(min(A_SEL_TOPK, n_sel)):
        top = jnp.max(imp, axis=0, keepdims=True)
        first = jnp.min(jnp.where(imp == top, jrow_f, float(n_sel)), axis=0, keepdims=True)
        pick = jrow_f == first
        selected = jnp.where(pick, 1.0, selected)
        imp = jnp.where(pick, -2.0, imp)
    selt_sc[...] = selected

    ones = jnp.ones((ONES_ROWS, t), BF16)
    blocks_per_tile = t // A_SEL_BLOCK

    def branch(kv_ref, vt_ref, keep_fn, last, out_ref):
        m_sc[...] = jnp.full(m_sc.shape, NEG_INF, F32)
        acc_sc[...] = jnp.zeros(acc_sc.shape, F32)

        def score(s, par):
            kv = kv_ref[pl.ds(pl.multiple_of((i - s) * t, t), t), :]
            for g in range(g_n):
                u_sc[par, g] = lax.dot_general(kv, qpad[g], _NT, preferred_element_type=F32)

        def softmax(s, par, diagonal):
            keep = keep_fn(s, diagonal)
            offset = jnp.asarray(-s * t, jnp.int32).astype(F32)
            for g in range(g_n):
                cj = slopes2[g] * offset
                u = jnp.where(keep, u_sc[par, g] + b0_sc[g], NEG_INF)
                m_old = m_sc[g]
                m_new = jnp.maximum(m_old, jnp.max(u, axis=0, keepdims=True) + cj)
                p_sc[par, g] = jnp.exp2(u - (m_new - cj)).astype(BF16)
                a_sc[par, g] = jnp.exp2(m_old - m_new)
                m_sc[g] = m_new

        def accumulate(s, par):
            vaug = jnp.concatenate([vt_ref[0, 0, i - s], ones], axis=0)
            pv = [jnp.dot(vaug, p_sc[par, g], preferred_element_type=F32) for g in range(g_n)]
            for g in range(g_n):
                acc_sc[g] = a_sc[par, g] * acc_sc[g] + pv[g]

        _pipeline3(last, score, softmax, accumulate)
        store_token_major(out_ref, [acc_sc[g, :dh] / jnp.maximum(acc_sc[g, dh:dh + 1], TINY)
                                    for g in range(g_n)])

    def keep_selected(s, diagonal):
        first_block = (i - s) * blocks_per_tile
        keep = jnp.concatenate(
            [jnp.broadcast_to(selt_sc[pl.ds(first_block + b, 1), :], (A_SEL_BLOCK, t))
             for b in range(blocks_per_tile)], axis=0) > 0.5
        if diagonal:
            keep = keep & (_row_iota((t, t)) <= _lane_iota((t, t)))
        return keep

    def keep_window(s, diagonal):
        rel = _row_iota((t, t)) - _lane_iota((t, t))
        return rel <= 0 if diagonal else rel > s * t - A_WINDOW

    branch(sel_ref, vst_ref, keep_selected, i, os_ref)
    branch(win_ref, vwt_ref, keep_window, jnp.minimum(i, A_WINDOW // t), ow_ref)


def _nsa_attention(proj, cmp_kv, batch, seq):
    t = A_T
    nq = seq // t
    hk_n, dh, g_n = A_KV_HEADS, HEAD_DIM, A_GROUP
    n_cmp = cmp_kv.shape[2]
    n_sel = seq // A_SEL_BLOCK
    ov = np.zeros((n_cmp, n_sel), np.float32)
    real = (seq - A_CMP_BLOCK) // A_CMP_STRIDE + 1
    cidx = A_CMP_STRIDE * np.arange(real)[:, None] + np.arange(A_CMP_BLOCK)[None, :]
    np.add.at(ov, (np.repeat(np.arange(real), A_CMP_BLOCK), (cidx // A_SEL_BLOCK).ravel()), 1.0 / A_CMP_BLOCK)
    qb = D_MODEL // (g_n * dh)
    sb = 2 * D_MODEL // LANES
    wb = sb + hk_n
    c_sel, c_win = 2 * D_MODEL, 2 * D_MODEL + 2 * hk_n * dh

    def values_transposed(c0):
        v = proj[:, c0:c0 + 2 * hk_n * dh].reshape(-1, hk_n, 2, dh)[:, :, 1, :]
        return _tile_transposed(v.reshape(-1, hk_n * dh), batch, hk_n, dh, t)

    vct = cmp_kv[..., dh:].transpose(0, 1, 3, 2)
    vt_spec = pl.BlockSpec((1, 1, nq, dh, t), lambda b, h, i: (b, h, 0, 0, 0))
    o_spec = pl.BlockSpec((t, g_n * dh), lambda b, h, i: (b * nq + i, h))
    o_shape = jax.ShapeDtypeStruct((batch * seq, D_MODEL), BF16)
    return pl.pallas_call(
        _nsa_body,
        grid=(batch, hk_n, nq),
        in_specs=[_smem_spec(),
                  pl.BlockSpec((t, g_n * dh), lambda b, h, i: (b * nq + i, qb + h)),
                  pl.BlockSpec((seq, LANES), lambda b, h, i: (b, sb + h)),
                  vt_spec,
                  pl.BlockSpec((seq, LANES), lambda b, h, i: (b, wb + h)),
                  vt_spec,
                  pl.BlockSpec((1, 1, n_cmp, LANES), lambda b, h, i: (b, h, 0, 0)),
                  pl.BlockSpec((1, 1, dh, n_cmp), lambda b, h, i: (b, h, 0, 0)),
                  pl.BlockSpec((n_sel, n_cmp), lambda b, h, i: (0, 0))],
        out_specs=[o_spec, o_spec, o_spec],
        out_shape=[o_shape, o_shape, o_shape],
        scratch_shapes=[pltpu.VMEM((g_n, 1, t), F32),
                        pltpu.VMEM((g_n, dh + ONES_ROWS, t), F32),
                        pltpu.VMEM((g_n, t, t), F32),
                        pltpu.VMEM((n_sel, t), F32),
                        pltpu.VMEM((2, g_n, t, t), F32),
                        pltpu.VMEM((2, g_n, t, t), BF16),
                        pltpu.VMEM((2, g_n, 1, t), F32)],
        compiler_params=_params(3),
        name="nsa_attention",
    )(jnp.asarray(_alibi_slopes(A_HEADS)), proj, proj, values_transposed(c_sel), proj,
      values_transposed(c_win), cmp_kv, vct, jnp.asarray(ov.T, BF16))


def _nsa_layer(h, batch, seq, gain_pre, gain_post, w_in, w_out, pos_k, pos_v, w1_k, w2_k, w1_v, w2_v):
    dm, dh, hk_n = D_MODEL, HEAD_DIM, A_KV_HEADS
    kvw = hk_n * dh
    q0, kc0, vc0, ks0, vs0, kw0, vw0, gl0, z0 = np.cumsum((0, dm) + (kvw,) * 6 + (3 * A_HEADS,))
    cols = [np.arange(z0, z0 + dm), np.arange(q0, q0 + dm)]
    for k0, v0 in ((ks0, vs0), (kw0, vw0)):
        for hh in range(hk_n):
            cols += [np.arange(k0 + hh * dh, k0 + (hh + 1) * dh), np.arange(v0 + hh * dh, v0 + (hh + 1) * dh)]
    cols += [np.arange(kc0, kc0 + kvw), np.arange(vc0, vc0 + kvw), np.arange(gl0, gl0 + 3 * A_HEADS)]
    cols = np.concatenate(cols)
    w = jnp.pad(w_in[:, cols], ((0, 0), (0, LANES - 3 * A_HEADS))).astype(BF16)
    proj = _proj_in(h, gain_pre, w)

    grp = A_CMP_STRIDE
    cmp_col = 2 * dm + 4 * kvw

    def groups(c0):
        a = proj[:, c0:c0 + kvw].reshape(batch, seq // grp, grp, hk_n, dh)
        return a.transpose(0, 3, 1, 2, 4).reshape(batch, hk_n, seq // grp, grp * dh)

    half = grp * dh
    pos = jnp.stack([pos_k.reshape(2, 1, half), pos_v.reshape(2, 1, half)])
    w1 = jnp.stack([w1_k.reshape(2, half, A_CMP_HIDDEN), w1_v.reshape(2, half, A_CMP_HIDDEN)]).astype(BF16)
    w2 = jnp.stack([jnp.pad(w2_k, ((0, 0), (0, dh))), jnp.pad(w2_v, ((0, 0), (dh, 0)))]).astype(BF16)
    cmp_kv = _compress(groups(cmp_col), groups(cmp_col + kvw), pos, w1, w2)

    oc, osel, ow = _nsa_attention(proj, cmp_kv, batch, seq)
    gl_blk = (cmp_col + 2 * kvw) // LANES
    return _proj_out([oc, osel, ow], proj, 0, w_out.astype(BF16), gain_post, h, gl_blk=gl_blk)


def kernel(x, norm_pre, norm_post, a_w_in, a_w_out, a_cmp_pos_k, a_cmp_pos_v, a_cmp_w1_k, a_cmp_w2_k,
           a_cmp_w1_v, a_cmp_w2_v, b_w_in, b_w_out, b_lambda, b_sub_gain, c_w_in, c_w_out, d_w_in, d_w_out):
    batch, seq, dm = x.shape
    h = x.reshape(batch * seq, dm)
    for i in range(DEPTH):
        mixer, j = i % 4, i // 4
        if mixer == 0:
            h = _nsa_layer(h, batch, seq, norm_pre[i], norm_post[i], a_w_in[j], a_w_out[j],
                           a_cmp_pos_k[j], a_cmp_pos_v[j], a_cmp_w1_k[j], a_cmp_w2_k[j],
                           a_cmp_w1_v[j], a_cmp_w2_v[j])
            continue
        if mixer == 1:
            proj = _proj_in(h, norm_pre[i], b_w_in[j].astype(BF16))
            lambda_init = 0.8 - 0.6 * math.exp(-0.3 * i)
            o = _diff_attention(proj, b_lambda[j], b_sub_gain[j], lambda_init, batch, seq)
            z_blk, w_out = 3, b_w_out[j]
        elif mixer == 2:
            qk = 2 * len(C_PATTERNS) * C_HEADS * HEAD_DIM
            w = jnp.concatenate([c_w_in[j][:, qk + dm:], c_w_in[j][:, :qk + dm]], axis=1)
            proj = _proj_in(h, norm_pre[i], w.astype(BF16))
            o = _dilated_attention(proj, batch, seq)
            z_blk, w_out = 0, c_w_out[j]
        else:
            proj = _proj_in(h, norm_pre[i], d_w_in[j].astype(BF16))
            o = _stick_attention(proj, batch, seq)
            z_blk, w_out = 3, d_w_out[j]
        h = _proj_out([o], proj, z_blk, w_out.astype(BF16), norm_post[i], h)
    return h.reshape(batch, seq, dm)
```

```python
import functools
import math

import jax
import jax.numpy as jnp
import numpy as np
from jax import lax
from jax.experimental import pallas as pl
from jax.experimental.pallas import tpu as pltpu

F32 = jnp.float32
BF16 = jnp.bfloat16

D_MODEL = 1024
HEAD_DIM = 64
DEPTH = 4
RMS_EPS = 1e-6
NEG_INF = -1e30
TINY = 1e-30
LANES = 128
ONES_ROWS = 16
LOG2E = math.log2(math.e)
VMEM_LIMIT = 48 * 1024 * 1024

A_HEADS = 16
A_KV_HEADS = 4
A_GROUP = 4
A_CMP_BLOCK = 32
A_CMP_STRIDE = 16
A_CMP_HIDDEN = 256
A_SEL_BLOCK = 64
A_SEL_SHIFT = 6
A_SEL_TOPK = 16
A_WINDOW = 512
A_FORCE_BONUS = 1e3
A_T = 256
A_FEAT_ROWS = 16
B_HEADS = 8
B_T = 256
C_PATTERNS = ((128, 1), (512, 4), (2048, 16))
C_HEADS = 4
C_V_DIM = 256
C_TQ = 128
D_HEADS = 16
D_T = 256

PROJ_TM = 512
PROJ_CHUNK = 512

_NT = (((1,), (1,)), ((), ()))


def _alibi_slopes(n):
    return np.array([2.0 ** (-8.0 * (i + 1) / n) for i in range(n)], np.float32)


def _params(n_grid):
    return pltpu.CompilerParams(dimension_semantics=("arbitrary",) * n_grid,
                                vmem_limit_bytes=VMEM_LIMIT)


def _split_bf16(x):
    hi = x.astype(BF16)
    lo = (x - hi.astype(F32)).astype(BF16)
    return hi, lo


def _sigmoid(x):
    return 1.0 / (1.0 + jnp.exp(-x))


def _proj_in_body(h_ref, g_ref, w_ref, o_ref):
    x = h_ref[...]
    ms = jnp.mean(x * x, axis=-1, keepdims=True)
    u = (x * lax.rsqrt(ms + RMS_EPS) * g_ref[...]).astype(BF16)
    n = o_ref.shape[1]
    for c in range(0, n, PROJ_CHUNK):
        e = min(c + PROJ_CHUNK, n)
        o_ref[:, c:e] = jnp.dot(u, w_ref[:, c:e], preferred_element_type=F32).astype(BF16)


def _proj_in(h, gain, w):
    m, n = h.shape[0], w.shape[1]
    return pl.pallas_call(
        _proj_in_body,
        grid=(m // PROJ_TM,),
        in_specs=[pl.BlockSpec((PROJ_TM, D_MODEL), lambda i: (i, 0)),
                  pl.BlockSpec((1, D_MODEL), lambda i: (0, 0)),
                  pl.BlockSpec((D_MODEL, n), lambda i: (0, 0))],
        out_specs=pl.BlockSpec((PROJ_TM, n), lambda i: (i, 0)),
        out_shape=jax.ShapeDtypeStruct((m, n), BF16),
        compiler_params=_params(1),
        name="proj_in",
    )(h, gain.reshape(1, D_MODEL), w)


def _finish(o, z_ref, w_ref, g_ref, h_ref, out_ref):
    z = z_ref[...].astype(F32)
    gated = (o * (z * _sigmoid(z))).astype(BF16)
    y = jnp.dot(gated, w_ref[...], preferred_element_type=F32)
    ms = jnp.mean(y * y, axis=-1, keepdims=True)
    out_ref[...] = h_ref[...] + y * lax.rsqrt(ms + RMS_EPS) * g_ref[...]


def _proj_out_body(o_ref, z_ref, w_ref, g_ref, h_ref, out_ref):
    _finish(o_ref[...].astype(F32), z_ref, w_ref, g_ref, h_ref, out_ref)


def _proj_out_nsa_body(oc_ref, os_ref, ow_ref, gl_ref, eg_ref, z_ref, w_ref, g_ref, h_ref, out_ref):
    hi, lo = _split_bf16(_sigmoid(gl_ref[...].astype(F32)))
    eg = eg_ref[...]
    gates = jnp.dot(hi, eg, preferred_element_type=F32) + jnp.dot(lo, eg, preferred_element_type=F32)
    o = (gates[:, :D_MODEL] * oc_ref[...].astype(F32)
         + gates[:, D_MODEL:2 * D_MODEL] * os_ref[...].astype(F32)
         + gates[:, 2 * D_MODEL:] * ow_ref[...].astype(F32))
    _finish(o, z_ref, w_ref, g_ref, h_ref, out_ref)


def _proj_out(o_list, proj, z_blk, w_out, gain, h, gl_blk=None):
    m = h.shape[0]
    tm = PROJ_TM
    row = lambda i: (i, 0)
    fixed = lambda i: (0, 0)
    o_specs = [pl.BlockSpec((tm, D_MODEL), row) for _ in o_list]
    tail_specs = [pl.BlockSpec((tm, D_MODEL), lambda i: (i, z_blk)),
                  pl.BlockSpec((D_MODEL, D_MODEL), fixed),
                  pl.BlockSpec((1, D_MODEL), fixed),
                  pl.BlockSpec((tm, D_MODEL), row)]
    tail = [proj, w_out, gain.reshape(1, D_MODEL), h]
    if gl_blk is None:
        body, in_specs, args = _proj_out_body, o_specs + tail_specs, list(o_list) + tail
    else:
        eg = np.zeros((LANES, 3 * D_MODEL), np.float32)
        for br in range(3):
            for hd in range(A_HEADS):
                c0 = br * D_MODEL + hd * HEAD_DIM
                eg[br * A_HEADS + hd, c0:c0 + HEAD_DIM] = 1.0
        body = _proj_out_nsa_body
        in_specs = o_specs + [pl.BlockSpec((tm, LANES), lambda i: (i, gl_blk)),
                              pl.BlockSpec((LANES, 3 * D_MODEL), fixed)] + tail_specs
        args = list(o_list) + [proj, jnp.asarray(eg, BF16)] + tail
    return pl.pallas_call(
        body,
        grid=(m // tm,),
        in_specs=in_specs,
        out_specs=pl.BlockSpec((tm, D_MODEL), row),
        out_shape=jax.ShapeDtypeStruct((m, D_MODEL), F32),
        compiler_params=_params(1),
        name="proj_out",
    )(*args)


def _half_masked(q32):
    lane = lax.broadcasted_iota(jnp.int32, (1, LANES), 1)
    lo = jnp.where(lane < HEAD_DIM, q32, 0.0).astype(BF16)
    hi = jnp.where(lane >= HEAD_DIM, q32, 0.0).astype(BF16)
    return lo, hi


def _smem_spec():
    return pl.BlockSpec(memory_space=pltpu.SMEM)


def _tile_transposed(cols, batch, n_heads, width, t):
    seq = cols.shape[0] // batch
    return cols.reshape(batch, seq // t, t, n_heads, width).transpose(0, 3, 1, 4, 2)


def _row_iota(shape):
    return lax.broadcasted_iota(jnp.int32, shape, 0)


def _lane_iota(shape):
    return lax.broadcasted_iota(jnp.int32, shape, 1)


def _pipeline3(i, score, softmax, accumulate):
    score(0, 0)
    score(jnp.minimum(1, i), 1)
    softmax(0, 0, True)

    def pair(r, carry):
        s = 2 * r + 1
        accumulate(s - 1, 0)
        score(s + 1, 0)
        softmax(s, 1, False)
        accumulate(s, 1)
        score(jnp.minimum(s + 2, i), 1)
        softmax(s + 1, 0, False)
        return carry

    lax.fori_loop(0, i // 2, pair, 0)

    @pl.when(i % 2 == 1)
    def _():
        accumulate(i - 1, 0)
        softmax(i, 1, False)
        accumulate(i, 1)

    @pl.when(i % 2 == 0)
    def _():
        accumulate(i, 0)


def _diff_body(slopes_ref, q_ref, k_ref, kf_ref, vt_ref, lam_ref, sg_ref, o_ref,
               m_sc, acc_sc, qt_sc, u_sc, p_sc, a_sc, *, lambda_init):
    t = B_T
    dv = 2 * HEAD_DIM
    hd = pl.program_id(1)
    i = pl.program_id(2)
    slope2 = slopes_ref[hd] * LOG2E

    qt = (q_ref[...].astype(F32) * (HEAD_DIM ** -0.5 * LOG2E)).T
    first_half = _row_iota((LANES, t)) < HEAD_DIM
    slope_rows = jnp.zeros((ONES_ROWS, t), F32) + slope2
    hi = slope_rows.astype(BF16).astype(F32)
    lo = (slope_rows - hi).astype(BF16).astype(F32)
    feat_row = _row_iota((ONES_ROWS, t))
    feat = jnp.where(feat_row == 0, hi, jnp.where(feat_row == 1, lo, 0.0)).astype(BF16)
    for mp in range(2):
        qt_sc[mp, :LANES] = jnp.where(first_half == (mp == 0), qt, 0.0).astype(BF16)
        qt_sc[mp, LANES:LANES + ONES_ROWS] = feat
        qt_sc[mp, LANES + ONES_ROWS:] = jnp.zeros((LANES - ONES_ROWS, t), BF16)
    m_sc[...] = jnp.full(m_sc.shape, NEG_INF, F32)
    acc_sc[...] = jnp.zeros(acc_sc.shape, F32)
    ones = jnp.ones((ONES_ROWS, t), BF16)

    def score(s, par):
        k = k_ref[pl.ds(pl.multiple_of((i - s) * t, t), t), :]
        k_aug = jnp.concatenate([k, kf_ref[...]], axis=1)
        for mp in range(2):
            u_sc[par, mp] = jnp.dot(k_aug, qt_sc[mp], preferred_element_type=F32)

    def softmax(s, par, diagonal):
        cj = slope2 * jnp.asarray(-s * t, jnp.int32).astype(F32)
        for mp in range(2):
            for c in range(0, t, LANES):
                cols = slice(c, c + LANES)
                u = u_sc[par, mp, :, cols]
                if diagonal:
                    u = jnp.where(_row_iota((t, LANES)) <= c + _lane_iota((t, LANES)), u, NEG_INF)
                m_old = m_sc[mp, :, cols]
                m_new = jnp.maximum(m_old, jnp.max(u, axis=0, keepdims=True) + cj)
                p_sc[par, mp, :, cols] = jnp.exp2(u - (m_new - cj)).astype(BF16)
                a_sc[par, mp, :, cols] = jnp.exp2(m_old - m_new)
                m_sc[mp, :, cols] = m_new

    def accumulate(s, par):
        vaug = jnp.concatenate([vt_ref[0, 0, i - s], ones], axis=0)
        pv = [jnp.dot(vaug, p_sc[par, mp], preferred_element_type=F32) for mp in range(2)]
        for mp in range(2):
            acc_sc[mp] = a_sc[par, mp] * acc_sc[mp] + pv[mp]

    _pipeline3(i, score, softmax, accumulate)

    lam = lam_ref[...]
    lam_full = (jnp.exp(jnp.sum(lam[0:1] * lam[1:2], axis=1, keepdims=True))
                - jnp.exp(jnp.sum(lam[2:3] * lam[3:4], axis=1, keepdims=True)) + lambda_init)
    o0 = acc_sc[0, :dv] / jnp.maximum(acc_sc[0, dv:dv + 1], TINY)
    o1 = acc_sc[1, :dv] / jnp.maximum(acc_sc[1, dv:dv + 1], TINY)
    a = o0 - lam_full * o1
    ms = jnp.mean(a * a, axis=0, keepdims=True)
    y = (a * lax.rsqrt(ms + RMS_EPS) * sg_ref[...]) * (1.0 - lambda_init)
    o_ref[...] = y.T.astype(BF16)


def _key_row_features(t):
    kf = np.zeros((t, LANES), np.float32)
    kf[:, 0] = kf[:, 1] = np.arange(t)
    return jnp.asarray(kf, BF16)


def _diff_attention(proj, lam, sub_gain, lambda_init, batch, seq):
    t = B_T
    nq = seq // t
    nh = B_HEADS
    dv = 2 * HEAD_DIM
    vt = _tile_transposed(proj[:, 2 * D_MODEL:3 * D_MODEL], batch, nh, dv, t)
    return pl.pallas_call(
        functools.partial(_diff_body, lambda_init=lambda_init),
        grid=(batch, nh, nq),
        in_specs=[_smem_spec(),
                  pl.BlockSpec((t, LANES), lambda b, h, i: (b * nq + i, h)),
                  pl.BlockSpec((seq, LANES), lambda b, h, i: (b, nh + h)),
                  pl.BlockSpec((t, LANES), lambda b, h, i: (0, 0)),
                  pl.BlockSpec((1, 1, nq, dv, t), lambda b, h, i: (b, h, 0, 0, 0)),
                  pl.BlockSpec((4, HEAD_DIM), lambda b, h, i: (0, 0)),
                  pl.BlockSpec((dv, 1), lambda b, h, i: (0, 0))],
        out_specs=pl.BlockSpec((t, LANES), lambda b, h, i: (b * nq + i, h)),
        out_shape=jax.ShapeDtypeStruct((batch * seq, D_MODEL), BF16),
        scratch_shapes=[pltpu.VMEM((2, 1, t), F32), pltpu.VMEM((2, dv + ONES_ROWS, t), F32),
                        pltpu.VMEM((2, 2 * LANES, t), BF16),
                        pltpu.VMEM((2, 2, t, t), F32), pltpu.VMEM((2, 2, t, t), BF16),
                        pltpu.VMEM((2, 2, 1, t), F32)],
        compiler_params=_params(3),
        name="diff_attention",
    )(jnp.asarray(_alibi_slopes(nh)), proj, proj, _key_row_features(t), vt, lam, sub_gain.reshape(dv, 1))


def _softplus2(x):
    sign = jnp.uint32(0x80000000)
    neg_abs = lax.bitcast_convert_type(lax.bitcast_convert_type(x, jnp.uint32) | sign, F32)
    return jnp.maximum(x, 0.0) + jnp.log(1.0 + jnp.exp2(neg_abs)) * LOG2E


def _stick_body(q_ref, k_ref, vt_ref, tri_ref, o_ref, c_sc, acc_sc, u_sc, ls_sc, hl_sc, a_sc, f_sc):
    t = D_T
    dh = HEAD_DIM
    i = pl.program_id(2)
    qm = _half_masked(q_ref[...].astype(F32) * (dh ** -0.5 * LOG2E))
    c_sc[...] = jnp.zeros(c_sc.shape, F32)
    acc_sc[...] = jnp.zeros(acc_sc.shape, F32)
    a_sc[...] = jnp.zeros(a_sc.shape, BF16)
    f_sc[...] = jnp.zeros(f_sc.shape, F32)

    def logits(s, par):
        k = k_ref[pl.ds(pl.multiple_of((i - s) * t, t), t), :]
        for hh in range(2):
            u_sc[par, hh] = lax.dot_general(k, qm[hh], _NT, preferred_element_type=F32)

    def softplus(par, diagonal):
        for hh in range(2):
            logit = u_sc[par, hh]
            sp = _softplus2(logit)
            log_sig = logit - sp
            if diagonal:
                before = _row_iota((t, t)) < _lane_iota((t, t))
                sp = jnp.where(before, sp, 0.0)
                log_sig = jnp.where(before, log_sig, NEG_INF)
            ls_sc[par, hh] = log_sig
            hl_sc[par, hh] = sp.astype(BF16)

    def suffix_sums(par):
        tri = tri_ref[...]
        return [jnp.dot(tri, hl_sc[par, hh], preferred_element_type=F32) for hh in range(2)]

    def weights(sums, par):
        for hh in range(2):
            a_sc[par, hh] = jnp.exp2(ls_sc[par, hh] + sums[hh][:t]).astype(BF16)
            f_sc[par, hh] = jnp.exp2(c_sc[hh])
            c_sc[hh] = c_sc[hh] + sums[hh][t:t + 1]

    def values(s, par):
        vt = vt_ref[0, 0, jnp.minimum(i - s, i)]
        for hh in range(2):
            acc_sc[hh] = acc_sc[hh] + f_sc[par, hh] * jnp.dot(vt[hh * dh:(hh + 1) * dh], a_sc[par, hh],
                                                              preferred_element_type=F32)

    def iteration(s, par, prefetch):
        sums = suffix_sums(1 - par)
        values(s - 2, par)
        if prefetch:
            logits(jnp.minimum(s + 1, i), 1 - par)
        softplus(par, False)
        weights(sums, 1 - par)

    def drain(par):
        sums = suffix_sums(par)
        values(i - 1, 1 - par)
        weights(sums, par)
        values(i, par)

    logits(0, 0)
    logits(jnp.minimum(1, i), 1)
    softplus(0, True)

    def pair(r, carry):
        s = 2 * r + 1
        iteration(s, 1, True)
        iteration(s + 1, 0, True)
        return carry

    lax.fori_loop(0, i // 2, pair, 0)

    @pl.when(i % 2 == 1)
    def _():
        iteration(i, 1, False)
        drain(1)

    @pl.when(i % 2 == 0)
    def _():
        drain(0)

    o_ref[...] = jnp.concatenate([acc_sc[0], acc_sc[1]], axis=0).T.astype(BF16)


def _stick_attention(proj, batch, seq):
    t = D_T
    nq = seq // t
    nb = D_HEADS // 2
    tri = -np.concatenate([np.triu(np.ones((t, t), np.float32), 1),
                           np.ones((ONES_ROWS, t), np.float32)])
    vt = _tile_transposed(proj[:, 2 * D_MODEL:3 * D_MODEL], batch, nb, LANES, t)
    return pl.pallas_call(
        _stick_body,
        grid=(batch, nb, nq),
        in_specs=[pl.BlockSpec((t, LANES), lambda b, h, i: (b * nq + i, h)),
                  pl.BlockSpec((seq, LANES), lambda b, h, i: (b, nb + h)),
                  pl.BlockSpec((1, 1, nq, LANES, t), lambda b, h, i: (b, h, 0, 0, 0)),
                  pl.BlockSpec((t + ONES_ROWS, t), lambda b, h, i: (0, 0))],
        out_specs=pl.BlockSpec((t, LANES), lambda b, h, i: (b * nq + i, h)),
        out_shape=jax.ShapeDtypeStruct((batch * seq, D_MODEL), BF16),
        scratch_shapes=[pltpu.VMEM((2, 1, t), F32), pltpu.VMEM((2, HEAD_DIM, t), F32),
                        pltpu.VMEM((2, 2, t, t), F32), pltpu.VMEM((2, 2, t, t), F32),
                        pltpu.VMEM((2, 2, t, t), BF16), pltpu.VMEM((2, 2, t, t), BF16),
                        pltpu.VMEM((2, 2, 1, t), F32)],
        compiler_params=_params(3),
        name="stick_attention",
    )(proj, proj, vt, jnp.asarray(tri, BF16))


def _dilated_body(slopes_ref, q0_ref, q1_ref, q2_ref, k0_ref, k1_ref, k2_ref, v_ref, o_ref):
    tq = C_TQ
    hp = pl.program_id(1)
    i = pl.program_id(2)
    seq = v_ref.shape[0]
    q_start = i * tq
    t_col = q_start + lax.broadcasted_iota(jnp.int32, (tq, 1), 0)
    q_refs = (q0_ref, q1_ref, q2_ref)
    k_refs = (k0_ref, k1_ref, k2_ref)
    for hh in range(2):
        outs, lses = [], []
        for g, (w, d) in enumerate(C_PATTERNS):
            span = min(w + tq, seq)
            k_start = pl.multiple_of(jnp.maximum(q_start + tq - span, 0), tq)
            qm = _half_masked(q_refs[g][...].astype(F32) * HEAD_DIM ** -0.5)[hh]
            kk = k_refs[g][pl.ds(k_start, span), :]
            vv = v_ref[pl.ds(k_start, span), hh * C_V_DIM:(hh + 1) * C_V_DIM]
            slope = slopes_ref[g * C_HEADS + hp * 2 + hh]
            dist = t_col - (k_start + lax.broadcasted_iota(jnp.int32, (1, span), 1))
            valid = (dist >= 0) & (dist <= w) & ((dist & (d - 1)) == 0)
            s = lax.dot_general(qm, kk, _NT, preferred_element_type=F32) - slope * dist.astype(F32)
            s = jnp.where(valid, s, NEG_INF)
            m = jnp.max(s, axis=1, keepdims=True)
            e = jnp.where(valid, jnp.exp(s - m), 0.0)
            l = jnp.sum(e, axis=1, keepdims=True)
            outs.append(jnp.dot(e.astype(BF16), vv, preferred_element_type=F32) / l)
            lses.append(m + jnp.log(l))
        mx = jnp.maximum(jnp.maximum(lses[0], lses[1]), lses[2])
        ws = [jnp.exp(x - mx) for x in lses]
        tot = ws[0] + ws[1] + ws[2]
        mixed = (ws[0] * outs[0] + ws[1] * outs[1] + ws[2] * outs[2]) / tot
        o_ref[:, hh * C_V_DIM:(hh + 1) * C_V_DIM] = mixed.astype(BF16)


def _dilated_attention(proj, batch, seq):
    tq = C_TQ
    nq = seq // tq
    zb = D_MODEL // LANES
    nqk = len(C_PATTERNS) * C_HEADS // 2
    vb = (D_MODEL + 2 * nqk * LANES) // (2 * C_V_DIM)
    q_spec = lambda g: pl.BlockSpec((tq, LANES), lambda b, h, i: (b * nq + i, zb + 2 * g + h))
    k_spec = lambda g: pl.BlockSpec((seq, LANES), lambda b, h, i: (b, zb + nqk + 2 * g + h))
    return pl.pallas_call(
        _dilated_body,
        grid=(batch, 2, nq),
        in_specs=[_smem_spec(), q_spec(0), q_spec(1), q_spec(2), k_spec(0), k_spec(1), k_spec(2),
                  pl.BlockSpec((seq, 2 * C_V_DIM), lambda b, h, i: (b, vb + h))],
        out_specs=pl.BlockSpec((tq, 2 * C_V_DIM), lambda b, h, i: (b * nq + i, h)),
        out_shape=jax.ShapeDtypeStruct((batch * seq, D_MODEL), BF16),
        compiler_params=_params(3),
        name="dilated_attention",
    )(jnp.asarray(_alibi_slopes(len(C_PATTERNS) * C_HEADS)), *([proj] * 7))


def _gelu_tanh(x):
    return 0.5 * x * (1.0 + jnp.tanh(math.sqrt(2.0 / math.pi) * (x + 0.044715 * x * x * x)))


def _compress_body(ak_ref, av_ref, pos_ref, w1_ref, w2_ref, o_ref):
    n = ak_ref.shape[2]
    acc = jnp.zeros((n, LANES), F32)
    for kv, a_ref in enumerate((ak_ref, av_ref)):
        a = a_ref[0, 0].astype(F32)
        first = jnp.dot((a + pos_ref[kv, 0]).astype(BF16), w1_ref[kv, 0], preferred_element_type=F32)
        second = jnp.dot((a + pos_ref[kv, 1]).astype(BF16), w1_ref[kv, 1], preferred_element_type=F32)
        hidden = _gelu_tanh(first + pltpu.roll(second, n - 1, 0))
        acc = acc + jnp.dot(hidden.astype(BF16), w2_ref[kv], preferred_element_type=F32)
    o_ref[0, 0] = acc.astype(BF16)


def _compress(ak, av, pos, w1, w2):
    batch, hk, n, width = ak.shape
    a_spec = pl.BlockSpec((1, 1, n, width), lambda b, h: (b, h, 0, 0))
    return pl.pallas_call(
        _compress_body,
        grid=(batch, hk),
        in_specs=[a_spec, a_spec,
                  pl.BlockSpec(pos.shape, lambda b, h: (0, 0, 0, 0)),
                  pl.BlockSpec(w1.shape, lambda b, h: (0, 0, 0, 0)),
                  pl.BlockSpec(w2.shape, lambda b, h: (0, 0, 0))],
        out_specs=pl.BlockSpec((1, 1, n, LANES), lambda b, h: (b, h, 0, 0)),
        out_shape=jax.ShapeDtypeStruct((batch, hk, n, LANES), BF16),
        compiler_params=_params(2),
        name="nsa_compress",
    )(ak, av, pos, w1, w2)


def _nsa_body(slopes_ref, q_ref, sel_ref, vst_ref, win_ref, vwt_ref, cmp_ref, vct_ref, ovt_ref, kf_ref,
              oc_ref, os_ref, ow_ref, m_sc, acc_sc, qt_sc, neg_sc, u_sc, p_sc, a_sc):
    t, g_n, dh = A_T, A_GROUP, HEAD_DIM
    hk = pl.program_id(1)
    i = pl.program_id(2)
    q_start = i * t
    slopes2 = [slopes_ref[hk * g_n + g] * LOG2E for g in range(g_n)]

    lane = _lane_iota((1, LANES))
    t_lane = q_start + _lane_iota((1, t))

    q32 = q_ref[...].astype(F32) * (dh ** -0.5 * LOG2E)
    qpad = []
    for g in range(g_n):
        blk = q32[:, (g // 2) * LANES:(g // 2 + 1) * LANES]
        if g % 2:
            blk = pltpu.roll(blk, dh, 1)
        qpad.append(jnp.where(lane < dh, blk, 0.0).astype(BF16))

    qt = q32.T
    feat_row = _row_iota((A_FEAT_ROWS // 2, t))
    slope_rows = []
    for g in range(g_n):
        sr = jnp.zeros((A_FEAT_ROWS // 2, t), F32) + slopes2[g]
        hi = sr.astype(BF16).astype(F32)
        lo = (sr - hi).astype(BF16).astype(F32)
        slope_rows.append(jnp.where(feat_row == 0, hi, jnp.where(feat_row == 1, lo, 0.0)))
        for par in range(2):
            qt_sc[par, g, :dh] = qt[g * dh:(g + 1) * dh].astype(BF16)
            qt_sc[par, g, dh + A_FEAT_ROWS:] = jnp.zeros((LANES - dh - A_FEAT_ROWS, t), BF16)

    def store_token_major(ref, per_head):
        ref[...] = jnp.concatenate(per_head, axis=0).T.astype(BF16)

    n_cmp = cmp_ref.shape[2]
    ckv = cmp_ref[0, 0]
    vct = vct_ref[0, 0]
    cmp_end = A_CMP_STRIDE * _row_iota((n_cmp, t)) + (A_CMP_BLOCK - 1)
    cvalid = t_lane >= cmp_end
    cpos = (cmp_end - q_start).astype(F32)
    raw = [lax.dot_general(ckv, qpad[g], _NT, preferred_element_type=F32) for g in range(g_n)]
    probs = []
    for g in range(g_n):
        u = jnp.where(cvalid, raw[g] + slopes2[g] * cpos, NEG_INF)
        e = jnp.where(cvalid, jnp.exp2(u - jnp.max(u, axis=0, keepdims=True)), 0.0)
        probs.append(e * (1.0 / jnp.maximum(jnp.sum(e, axis=0, keepdims=True), TINY)))
    store_token_major(oc_ref, [jnp.dot(vct, p.astype(BF16), preferred_element_type=F32) for p in probs])
    p_sum = (probs[0] + probs[1]) + (probs[2] + probs[3])

    n_sel = ovt_ref.shape[0]
    hi, lo = _split_bf16(p_sum)
    ovt = ovt_ref[...]
    imp = jnp.dot(ovt, hi, preferred_element_type=F32) + jnp.dot(ovt, lo, preferred_element_type=F32)
    jrow = _row_iota((n_sel, t))
    cur = jnp.right_shift(t_lane, A_SEL_SHIFT)
    forced = (jrow == 0) | (jrow == cur) | (jrow == cur - 1)
    imp = jnp.where(jrow > cur, -1.0, imp + jnp.where(forced, A_FORCE_BONUS, 0.0))
    jrow_f = jrow.astype(F32)
    selected = jnp.zeros((n_sel, t), F32)
    for _ in range(min(A_SEL_TOPK, n_sel)):
        top = jnp.max(imp, axis=0, keepdims=True)
        first = jnp.min(jnp.where(imp == top, jrow_f, float(n_sel)), axis=0, keepdims=True)
        pick = jrow_f == first
        selected = jnp.where(pick, 1.0, selected)
        imp = jnp.where(pick, -2.0, imp)
    neg_sc[...] = jnp.where(selected > 0.5, 0.0, NEG_INF)

    ones = jnp.ones((ONES_ROWS, t), BF16)
    blocks_per_pair = 2 * t // A_SEL_BLOCK

    def branch(kv_ref, vt_ref, block_masked, keep_fn, last, out_ref):
        m_sc[...] = jnp.full(m_sc.shape, NEG_INF, F32)
        acc_sc[...] = jnp.zeros(acc_sc.shape, F32)

        def set_features(par, mask_rows):
            for g in range(g_n):
                qt_sc[par, g, dh:dh + A_FEAT_ROWS] = jnp.concatenate([slope_rows[g], mask_rows], axis=0).astype(BF16)

        if not block_masked:
            for par in range(2):
                set_features(par, jnp.zeros((A_FEAT_ROWS // 2, t), F32))

        def score(s, par):
            tile = i - s
            kv = kv_ref[pl.ds(pl.multiple_of(tile * t, t), t), :]
            k_aug = jnp.where(lane < dh, kv, kf_ref[tile % 2])
            if block_masked:
                pair_start = pl.multiple_of((tile // 2) * blocks_per_pair, blocks_per_pair)
                set_features(par, neg_sc[pl.ds(pair_start, blocks_per_pair), :])
            for g in range(g_n):
                u_sc[par, g] = jnp.dot(k_aug, qt_sc[par, g], preferred_element_type=F32)

        def softmax(s, par, diagonal):
            keep = keep_fn(s, diagonal)
            offset = jnp.asarray(-s * t, jnp.int32).astype(F32)
            for g in range(g_n):
                cj = slopes2[g] * offset
                u = u_sc[par, g]
                if keep is not None:
                    u = jnp.where(keep, u, NEG_INF)
                m_old = m_sc[g]
                m_new = jnp.maximum(m_old, jnp.max(u, axis=0, keepdims=True) + cj)
                p_sc[par, g] = jnp.exp2(u - (m_new - cj)).astype(BF16)
                a_sc[par, g] = jnp.exp2(m_old - m_new)
                m_sc[g] = m_new

        def accumulate(s, par):
            vaug = jnp.concatenate([vt_ref[0, 0, i - s], ones], axis=0)
            pv = [jnp.dot(vaug, p_sc[par, g], preferred_element_type=F32) for g in range(g_n)]
            for g in range(g_n):
                acc_sc[g] = a_sc[par, g] * acc_sc[g] + pv[g]

        _pipeline3(last, score, softmax, accumulate)
        store_token_major(out_ref, [acc_sc[g, :dh] / jnp.maximum(acc_sc[g, dh:dh + 1], TINY)
                                    for g in range(g_n)])

    def keep_selected(s, diagonal):
        return _row_iota((t, t)) <= _lane_iota((t, t)) if diagonal else None

    def keep_window(s, diagonal):
        rel = _row_iota((t, t)) - _lane_iota((t, t))
        return rel <= 0 if diagonal else rel > s * t - A_WINDOW

    branch(sel_ref, vst_ref, True, keep_selected, i, os_ref)
    branch(win_ref, vwt_ref, False, keep_window, jnp.minimum(i, A_WINDOW // t), ow_ref)


def _nsa_attention(proj, cmp_kv, batch, seq):
    t = A_T
    nq = seq // t
    hk_n, dh, g_n = A_KV_HEADS, HEAD_DIM, A_GROUP
    n_cmp = cmp_kv.shape[2]
    n_sel = seq // A_SEL_BLOCK
    ov = np.zeros((n_cmp, n_sel), np.float32)
    real = (seq - A_CMP_BLOCK) // A_CMP_STRIDE + 1
    cidx = A_CMP_STRIDE * np.arange(real)[:, None] + np.arange(A_CMP_BLOCK)[None, :]
    np.add.at(ov, (np.repeat(np.arange(real), A_CMP_BLOCK), (cidx // A_SEL_BLOCK).ravel()), 1.0 / A_CMP_BLOCK)
    qb = D_MODEL // (g_n * dh)
    sb = 2 * D_MODEL // LANES
    wb = sb + hk_n
    c_sel, c_win = 2 * D_MODEL, 2 * D_MODEL + 2 * hk_n * dh

    def values_transposed(c0):
        v = proj[:, c0:c0 + 2 * hk_n * dh].reshape(-1, hk_n, 2, dh)[:, :, 1, :]
        return _tile_transposed(v.reshape(-1, hk_n * dh), batch, hk_n, dh, t)

    vct = cmp_kv[..., dh:].transpose(0, 1, 3, 2)
    kf = np.zeros((2, t, LANES), np.float32)
    kf[:, :, dh] = kf[:, :, dh + 1] = np.arange(t)
    for par in range(2):
        blk = par * (t // A_SEL_BLOCK) + np.arange(t) // A_SEL_BLOCK
        kf[par, np.arange(t), dh + A_FEAT_ROWS // 2 + blk] = 1.0
    vt_spec = pl.BlockSpec((1, 1, nq, dh, t), lambda b, h, i: (b, h, 0, 0, 0))
    o_spec = pl.BlockSpec((t, g_n * dh), lambda b, h, i: (b * nq + i, h))
    o_shape = jax.ShapeDtypeStruct((batch * seq, D_MODEL), BF16)
    return pl.pallas_call(
        _nsa_body,
        grid=(batch, hk_n, nq),
        in_specs=[_smem_spec(),
                  pl.BlockSpec((t, g_n * dh), lambda b, h, i: (b * nq + i, qb + h)),
                  pl.BlockSpec((seq, LANES), lambda b, h, i: (b, sb + h)),
                  vt_spec,
                  pl.BlockSpec((seq, LANES), lambda b, h, i: (b, wb + h)),
                  vt_spec,
                  pl.BlockSpec((1, 1, n_cmp, LANES), lambda b, h, i: (b, h, 0, 0)),
                  pl.BlockSpec((1, 1, dh, n_cmp), lambda b, h, i: (b, h, 0, 0)),
                  pl.BlockSpec((n_sel, n_cmp), lambda b, h, i: (0, 0)),
                  pl.BlockSpec((2, t, LANES), lambda b, h, i: (0, 0, 0))],
        out_specs=[o_spec, o_spec, o_spec],
        out_shape=[o_shape, o_shape, o_shape],
        scratch_shapes=[pltpu.VMEM((g_n, 1, t), F32),
                        pltpu.VMEM((g_n, dh + ONES_ROWS, t), F32),
                        pltpu.VMEM((2, g_n, LANES, t), BF16),
                        pltpu.VMEM((n_sel, t), F32),
                        pltpu.VMEM((2, g_n, t, t), F32),
                        pltpu.VMEM((2, g_n, t, t), BF16),
                        pltpu.VMEM((2, g_n, 1, t), F32)],
        compiler_params=_params(3),
        name="nsa_attention",
    )(jnp.asarray(_alibi_slopes(A_HEADS)), proj, proj, values_transposed(c_sel), proj,
      values_transposed(c_win), cmp_kv, vct, jnp.asarray(ov.T, BF16), jnp.asarray(kf, BF16))


def _nsa_layer(h, batch, seq, gain_pre, gain_post, w_in, w_out, pos_k, pos_v, w1_k, w2_k, w1_v, w2_v):
    dm, dh, hk_n = D_MODEL, HEAD_DIM, A_KV_HEADS
    kvw = hk_n * dh
    q0, kc0, vc0, ks0, vs0, kw0, vw0, gl0, z0 = np.cumsum((0, dm) + (kvw,) * 6 + (3 * A_HEADS,))
    cols = [np.arange(z0, z0 + dm), np.arange(q0, q0 + dm)]
    for k0, v0 in ((ks0, vs0), (kw0, vw0)):
        for hh in range(hk_n):
            cols += [np.arange(k0 + hh * dh, k0 + (hh + 1) * dh), np.arange(v0 + hh * dh, v0 + (hh + 1) * dh)]
    cols += [np.arange(kc0, kc0 + kvw), np.arange(vc0, vc0 + kvw), np.arange(gl0, gl0 + 3 * A_HEADS)]
    cols = np.concatenate(cols)
    w = jnp.pad(w_in[:, cols], ((0, 0), (0, LANES - 3 * A_HEADS))).astype(BF16)
    proj = _proj_in(h, gain_pre, w)

    grp = A_CMP_STRIDE
    cmp_col = 2 * dm + 4 * kvw

    def groups(c0):
        a = proj[:, c0:c0 + kvw].reshape(batch, seq // grp, grp, hk_n, dh)
        return a.transpose(0, 3, 1, 2, 4).reshape(batch, hk_n, seq // grp, grp * dh)

    half = grp * dh
    pos = jnp.stack([pos_k.reshape(2, 1, half), pos_v.reshape(2, 1, half)])
    w1 = jnp.stack([w1_k.reshape(2, half, A_CMP_HIDDEN), w1_v.reshape(2, half, A_CMP_HIDDEN)]).astype(BF16)
    w2 = jnp.stack([jnp.pad(w2_k, ((0, 0), (0, dh))), jnp.pad(w2_v, ((0, 0), (dh, 0)))]).astype(BF16)
    cmp_kv = _compress(groups(cmp_col), groups(cmp_col + kvw), pos, w1, w2)

    oc, osel, ow = _nsa_attention(proj, cmp_kv, batch, seq)
    gl_blk = (cmp_col + 2 * kvw) // LANES
    return _proj_out([oc, osel, ow], proj, 0, w_out.astype(BF16), gain_post, h, gl_blk=gl_blk)


def kernel(x, norm_pre, norm_post, a_w_in, a_w_out, a_cmp_pos_k, a_cmp_pos_v, a_cmp_w1_k, a_cmp_w2_k,
           a_cmp_w1_v, a_cmp_w2_v, b_w_in, b_w_out, b_lambda, b_sub_gain, c_w_in, c_w_out, d_w_in, d_w_out):
    batch, seq, dm = x.shape
    h = x.reshape(batch * seq, dm)
    for i in range(DEPTH):
        mixer, j = i % 4, i // 4
        if mixer == 0:
            h = _nsa_layer(h, batch, seq, norm_pre[i], norm_post[i], a_w_in[j], a_w_out[j],
                           a_cmp_pos_k[j], a_cmp_pos_v[j], a_cmp_w1_k[j], a_cmp_w2_k[j],
                           a_cmp_w1_v[j], a_cmp_w2_v[j])
            continue
        if mixer == 1:
            proj = _proj_in(h, norm_pre[i], b_w_in[j].astype(BF16))
            lambda_init = 0.8 - 0.6 * math.exp(-0.3 * i)
            o = _diff_attention(proj, b_lambda[j], b_sub_gain[j], lambda_init, batch, seq)
            z_blk, w_out = 3, b_w_out[j]
        elif mixer == 2:
            qk = 2 * len(C_PATTERNS) * C_HEADS * HEAD_DIM
            w = jnp.concatenate([c_w_in[j][:, qk + dm:], c_w_in[j][:, :qk + dm]], axis=1)
            proj = _proj_in(h, norm_pre[i], w.astype(BF16))
            o = _dilated_attention(proj, batch, seq)
            z_blk, w_out = 0, c_w_out[j]
        else:
            proj = _proj_in(h, norm_pre[i], d_w_in[j].astype(BF16))
            o = _stick_attention(proj, batch, seq)
            z_blk, w_out = 3, d_w_out[j]
        h = _proj_out([o], proj, z_blk, w_out.astype(BF16), norm_post[i], h)
    return h.reshape(batch, seq, dm)
```

```python
import functools
import math

import jax
import jax.numpy as jnp
import numpy as np
from jax import lax
from jax.experimental import pallas as pl
from jax.experimental.pallas import tpu as pltpu

F32 = jnp.float32
BF16 = jnp.bfloat16

D_MODEL = 1024
HEAD_DIM = 64
DEPTH = 4
RMS_EPS = 1e-6
NEG_INF = -1e30
TINY = 1e-30
LANES = 128
ONES_ROWS = 16
LOG2E = math.log2(math.e)
VMEM_LIMIT = 48 * 1024 * 1024

A_HEADS = 16
A_KV_HEADS = 4
A_GROUP = 4
A_CMP_BLOCK = 32
A_CMP_STRIDE = 16
A_CMP_HIDDEN = 256
A_SEL_BLOCK = 64
A_SEL_SHIFT = 6
A_SEL_TOPK = 16
A_WINDOW = 512
A_FORCE_BONUS = 1e3
A_T = 256
A_FEAT_ROWS = 16
B_HEADS = 8
B_T = 256
C_PATTERNS = ((128, 1), (512, 4), (2048, 16))
C_HEADS = 4
C_V_DIM = 256
C_TQ = 128
D_HEADS = 16
D_T = 256

PROJ_TM = 512
PROJ_CHUNK = 512

_NT = (((1,), (1,)), ((), ()))


def _alibi_slopes(n):
    return np.array([2.0 ** (-8.0 * (i + 1) / n) for i in range(n)], np.float32)


def _params(n_grid):
    return pltpu.CompilerParams(dimension_semantics=("arbitrary",) * n_grid,
                                vmem_limit_bytes=VMEM_LIMIT)


def _split_bf16(x):
    hi = x.astype(BF16)
    lo = (x - hi.astype(F32)).astype(BF16)
    return hi, lo


def _sigmoid(x):
    return 1.0 / (1.0 + jnp.exp(-x))


def _proj_in_body(h_ref, g_ref, w_ref, o_ref):
    x = h_ref[...]
    ms = jnp.mean(x * x, axis=-1, keepdims=True)
    u = (x * lax.rsqrt(ms + RMS_EPS) * g_ref[...]).astype(BF16)
    n = o_ref.shape[1]
    for c in range(0, n, PROJ_CHUNK):
        e = min(c + PROJ_CHUNK, n)
        o_ref[:, c:e] = jnp.dot(u, w_ref[:, c:e], preferred_element_type=F32).astype(BF16)


def _proj_in(h, gain, w):
    m, n = h.shape[0], w.shape[1]
    return pl.pallas_call(
        _proj_in_body,
        grid=(m // PROJ_TM,),
        in_specs=[pl.BlockSpec((PROJ_TM, D_MODEL), lambda i: (i, 0)),
                  pl.BlockSpec((1, D_MODEL), lambda i: (0, 0)),
                  pl.BlockSpec((D_MODEL, n), lambda i: (0, 0))],
        out_specs=pl.BlockSpec((PROJ_TM, n), lambda i: (i, 0)),
        out_shape=jax.ShapeDtypeStruct((m, n), BF16),
        compiler_params=_params(1),
        name="proj_in",
    )(h, gain.reshape(1, D_MODEL), w)


def _finish(o, z_ref, w_ref, g_ref, h_ref, out_ref):
    z = z_ref[...].astype(F32)
    gated = (o * (z * _sigmoid(z))).astype(BF16)
    y = jnp.dot(gated, w_ref[...], preferred_element_type=F32)
    ms = jnp.mean(y * y, axis=-1, keepdims=True)
    out_ref[...] = h_ref[...] + y * lax.rsqrt(ms + RMS_EPS) * g_ref[...]


def _proj_out_body(o_ref, z_ref, w_ref, g_ref, h_ref, out_ref):
    _finish(o_ref[...].astype(F32), z_ref, w_ref, g_ref, h_ref, out_ref)


def _proj_out_nsa_body(oc_ref, os_ref, ow_ref, gl_ref, eg_ref, z_ref, w_ref, g_ref, h_ref, out_ref):
    hi, lo = _split_bf16(_sigmoid(gl_ref[...].astype(F32)))
    eg = eg_ref[...]
    gates = jnp.dot(hi, eg, preferred_element_type=F32) + jnp.dot(lo, eg, preferred_element_type=F32)
    o = (gates[:, :D_MODEL] * oc_ref[...].astype(F32)
         + gates[:, D_MODEL:2 * D_MODEL] * os_ref[...].astype(F32)
         + gates[:, 2 * D_MODEL:] * ow_ref[...].astype(F32))
    _finish(o, z_ref, w_ref, g_ref, h_ref, out_ref)


def _proj_out(o_list, proj, z_blk, w_out, gain, h, gl_blk=None):
    m = h.shape[0]
    tm = PROJ_TM
    row = lambda i: (i, 0)
    fixed = lambda i: (0, 0)
    o_specs = [pl.BlockSpec((tm, D_MODEL), row) for _ in o_list]
    tail_specs = [pl.BlockSpec((tm, D_MODEL), lambda i: (i, z_blk)),
                  pl.BlockSpec((D_MODEL, D_MODEL), fixed),
                  pl.BlockSpec((1, D_MODEL), fixed),
                  pl.BlockSpec((tm, D_MODEL), row)]
    tail = [proj, w_out, gain.reshape(1, D_MODEL), h]
    if gl_blk is None:
        body, in_specs, args = _proj_out_body, o_specs + tail_specs, list(o_list) + tail
    else:
        eg = np.zeros((LANES, 3 * D_MODEL), np.float32)
        for br in range(3):
            for hd in range(A_HEADS):
                c0 = br * D_MODEL + hd * HEAD_DIM
                eg[br * A_HEADS + hd, c0:c0 + HEAD_DIM] = 1.0
        body = _proj_out_nsa_body
        in_specs = o_specs + [pl.BlockSpec((tm, LANES), lambda i: (i, gl_blk)),
                              pl.BlockSpec((LANES, 3 * D_MODEL), fixed)] + tail_specs
        args = list(o_list) + [proj, jnp.asarray(eg, BF16)] + tail
    return pl.pallas_call(
        body,
        grid=(m // tm,),
        in_specs=in_specs,
        out_specs=pl.BlockSpec((tm, D_MODEL), row),
        out_shape=jax.ShapeDtypeStruct((m, D_MODEL), F32),
        compiler_params=_params(1),
        name="proj_out",
    )(*args)


def _half_masked(q32):
    lane = lax.broadcasted_iota(jnp.int32, (1, LANES), 1)
    lo = jnp.where(lane < HEAD_DIM, q32, 0.0).astype(BF16)
    hi = jnp.where(lane >= HEAD_DIM, q32, 0.0).astype(BF16)
    return lo, hi


def _smem_spec():
    return pl.BlockSpec(memory_space=pltpu.SMEM)


def _tile_transposed(cols, batch, n_heads, width, t):
    seq = cols.shape[0] // batch
    return cols.reshape(batch, seq // t, t, n_heads, width).transpose(0, 3, 1, 4, 2)


def _row_iota(shape):
    return lax.broadcasted_iota(jnp.int32, shape, 0)


def _lane_iota(shape):
    return lax.broadcasted_iota(jnp.int32, shape, 1)


def _pipeline3(i, score, softmax, accumulate):
    score(0, 0)
    score(jnp.minimum(1, i), 1)
    softmax(0, 0, True)

    def pair(s):
        accumulate(s - 1, 0)
        score(s + 1, 0)
        softmax(s, 1, False)
        accumulate(s, 1)
        score(jnp.minimum(s + 2, i), 1)
        softmax(s + 1, 0, False)

    def quad_body(r, carry):
        pair(4 * r + 1)
        pair(4 * r + 3)
        return carry

    def pair_body(r, carry):
        pair(4 * (i // 4) + 2 * r + 1)
        return carry

    lax.fori_loop(0, i // 4, quad_body, 0)
    lax.fori_loop(0, (i % 4) // 2, pair_body, 0)

    @pl.when(i % 2 == 1)
    def _():
        accumulate(i - 1, 0)
        softmax(i, 1, False)
        accumulate(i, 1)

    @pl.when(i % 2 == 0)
    def _():
        accumulate(i, 0)


def _diff_body(slopes_ref, q_ref, k_ref, vt_ref, lam_ref, sg_ref, o_ref,
               m_sc, acc_sc, b0_sc, u_sc, p_sc, a_sc, *, lambda_init):
    t = B_T
    dv = 2 * HEAD_DIM
    hd = pl.program_id(1)
    i = pl.program_id(2)
    slope2 = slopes_ref[hd] * LOG2E

    @pl.when(i == 0)
    def _():
        b0_sc[...] = slope2 * _row_iota((t, t)).astype(F32)

    qm = _half_masked(q_ref[...].astype(F32) * (HEAD_DIM ** -0.5 * LOG2E))
    m_sc[...] = jnp.full(m_sc.shape, NEG_INF, F32)
    acc_sc[...] = jnp.zeros(acc_sc.shape, F32)
    ones = jnp.ones((ONES_ROWS, t), BF16)

    def score(s, par):
        k = k_ref[pl.ds(pl.multiple_of((i - s) * t, t), t), :]
        for mp in range(2):
            u_sc[par, mp] = lax.dot_general(k, qm[mp], _NT, preferred_element_type=F32)

    def softmax(s, par, diagonal):
        cj = slope2 * jnp.asarray(-s * t, jnp.int32).astype(F32)
        for mp in range(2):
            u = u_sc[par, mp] + b0_sc[...]
            if diagonal:
                u = jnp.where(_row_iota((t, t)) <= _lane_iota((t, t)), u, NEG_INF)
            m_old = m_sc[mp]
            m_new = jnp.maximum(m_old, jnp.max(u, axis=0, keepdims=True) + cj)
            p_sc[par, mp] = jnp.exp2(u - (m_new - cj)).astype(BF16)
            a_sc[par, mp] = jnp.exp2(m_old - m_new)
            m_sc[mp] = m_new

    def accumulate(s, par):
        vaug = jnp.concatenate([vt_ref[0, 0, i - s], ones], axis=0)
        pv = [jnp.dot(vaug, p_sc[par, mp], preferred_element_type=F32) for mp in range(2)]
        for mp in range(2):
            acc_sc[mp] = a_sc[par, mp] * acc_sc[mp] + pv[mp]

    _pipeline3(i, score, softmax, accumulate)

    lam = lam_ref[...]
    lam_full = (jnp.exp(jnp.sum(lam[0:1] * lam[1:2], axis=1, keepdims=True))
                - jnp.exp(jnp.sum(lam[2:3] * lam[3:4], axis=1, keepdims=True)) + lambda_init)
    o0 = acc_sc[0, :dv] / jnp.maximum(acc_sc[0, dv:dv + 1], TINY)
    o1 = acc_sc[1, :dv] / jnp.maximum(acc_sc[1, dv:dv + 1], TINY)
    a = o0 - lam_full * o1
    ms = jnp.mean(a * a, axis=0, keepdims=True)
    y = (a * lax.rsqrt(ms + RMS_EPS) * sg_ref[...]) * (1.0 - lambda_init)
    o_ref[...] = y.T.astype(BF16)


def _diff_attention(proj, lam, sub_gain, lambda_init, batch, seq):
    t = B_T
    nq = seq // t
    nh = B_HEADS
    dv = 2 * HEAD_DIM
    vt = _tile_transposed(proj[:, 2 * D_MODEL:3 * D_MODEL], batch, nh, dv, t)
    return pl.pallas_call(
        functools.partial(_diff_body, lambda_init=lambda_init),
        grid=(batch, nh, nq),
        in_specs=[_smem_spec(),
                  pl.BlockSpec((t, LANES), lambda b, h, i: (b * nq + i, h)),
                  pl.BlockSpec((seq, LANES), lambda b, h, i: (b, nh + h)),
                  pl.BlockSpec((1, 1, nq, dv, t), lambda b, h, i: (b, h, 0, 0, 0)),
                  pl.BlockSpec((4, HEAD_DIM), lambda b, h, i: (0, 0)),
                  pl.BlockSpec((dv, 1), lambda b, h, i: (0, 0))],
        out_specs=pl.BlockSpec((t, LANES), lambda b, h, i: (b * nq + i, h)),
        out_shape=jax.ShapeDtypeStruct((batch * seq, D_MODEL), BF16),
        scratch_shapes=[pltpu.VMEM((2, 1, t), F32), pltpu.VMEM((2, dv + ONES_ROWS, t), F32),
                        pltpu.VMEM((t, t), F32),
                        pltpu.VMEM((2, 2, t, t), F32), pltpu.VMEM((2, 2, t, t), BF16),
                        pltpu.VMEM((2, 2, 1, t), F32)],
        compiler_params=_params(3),
        name="diff_attention",
    )(jnp.asarray(_alibi_slopes(nh)), proj, proj, vt, lam, sub_gain.reshape(dv, 1))


def _softplus2(x):
    sign = jnp.uint32(0x80000000)
    neg_abs = lax.bitcast_convert_type(lax.bitcast_convert_type(x, jnp.uint32) | sign, F32)
    return jnp.maximum(x, 0.0) + jnp.log(1.0 + jnp.exp2(neg_abs)) * LOG2E


def _stick_body(q_ref, k_ref, vt_ref, tri_ref, o_ref, c_sc, acc_sc, u_sc, ls_sc, hl_sc, a_sc, f_sc):
    t = D_T
    dh = HEAD_DIM
    i = pl.program_id(2)
    qm = _half_masked(q_ref[...].astype(F32) * (dh ** -0.5 * LOG2E))
    c_sc[...] = jnp.zeros(c_sc.shape, F32)
    acc_sc[...] = jnp.zeros(acc_sc.shape, F32)
    a_sc[...] = jnp.zeros(a_sc.shape, BF16)
    f_sc[...] = jnp.zeros(f_sc.shape, F32)

    def logits(s, par):
        k = k_ref[pl.ds(pl.multiple_of((i - s) * t, t), t), :]
        for hh in range(2):
            u_sc[par, hh] = lax.dot_general(k, qm[hh], _NT, preferred_element_type=F32)

    def softplus(par, diagonal):
        for hh in range(2):
            logit = u_sc[par, hh]
            sp = _softplus2(logit)
            log_sig = logit - sp
            if diagonal:
                before = _row_iota((t, t)) < _lane_iota((t, t))
                sp = jnp.where(before, sp, 0.0)
                log_sig = jnp.where(before, log_sig, NEG_INF)
            ls_sc[par, hh] = log_sig
            hl_sc[par, hh] = sp.astype(BF16)

    def suffix_sums(par):
        tri = tri_ref[...]
        return [jnp.dot(tri, hl_sc[par, hh], preferred_element_type=F32) for hh in range(2)]

    def weights(sums, par):
        for hh in range(2):
            a_sc[par, hh] = jnp.exp2(ls_sc[par, hh] + sums[hh][:t]).astype(BF16)
            f_sc[par, hh] = jnp.exp2(c_sc[hh])
            c_sc[hh] = c_sc[hh] + sums[hh][t:t + 1]

    def values(s, par):
        vt = vt_ref[0, 0, jnp.minimum(i - s, i)]
        for hh in range(2):
            acc_sc[hh] = acc_sc[hh] + f_sc[par, hh] * jnp.dot(vt[hh * dh:(hh + 1) * dh], a_sc[par, hh],
                                                              preferred_element_type=F32)

    def iteration(s, par, prefetch):
        sums = suffix_sums(1 - par)
        values(s - 2, par)
        if prefetch:
            logits(jnp.minimum(s + 1, i), 1 - par)
        softplus(par, False)
        weights(sums, 1 - par)

    def drain(par):
        sums = suffix_sums(par)
        values(i - 1, 1 - par)
        weights(sums, par)
        values(i, par)

    logits(0, 0)
    logits(jnp.minimum(1, i), 1)
    softplus(0, True)

    def pair(s):
        iteration(s, 1, True)
        iteration(s + 1, 0, True)

    def quad_body(r, carry):
        pair(4 * r + 1)
        pair(4 * r + 3)
        return carry

    def pair_body(r, carry):
        pair(4 * (i // 4) + 2 * r + 1)
        return carry

    lax.fori_loop(0, i // 4, quad_body, 0)
    lax.fori_loop(0, (i % 4) // 2, pair_body, 0)

    @pl.when(i % 2 == 1)
    def _():
        iteration(i, 1, False)
        drain(1)

    @pl.when(i % 2 == 0)
    def _():
        drain(0)

    o_ref[...] = jnp.concatenate([acc_sc[0], acc_sc[1]], axis=0).T.astype(BF16)


def _stick_attention(proj, batch, seq):
    t = D_T
    nq = seq // t
    nb = D_HEADS // 2
    tri = -np.concatenate([np.triu(np.ones((t, t), np.float32), 1),
                           np.ones((ONES_ROWS, t), np.float32)])
    vt = _tile_transposed(proj[:, 2 * D_MODEL:3 * D_MODEL], batch, nb, LANES, t)
    return pl.pallas_call(
        _stick_body,
        grid=(batch, nb, nq),
        in_specs=[pl.BlockSpec((t, LANES), lambda b, h, i: (b * nq + i, h)),
                  pl.BlockSpec((seq, LANES), lambda b, h, i: (b, nb + h)),
                  pl.BlockSpec((1, 1, nq, LANES, t), lambda b, h, i: (b, h, 0, 0, 0)),
                  pl.BlockSpec((t + ONES_ROWS, t), lambda b, h, i: (0, 0))],
        out_specs=pl.BlockSpec((t, LANES), lambda b, h, i: (b * nq + i, h)),
        out_shape=jax.ShapeDtypeStruct((batch * seq, D_MODEL), BF16),
        scratch_shapes=[pltpu.VMEM((2, 1, t), F32), pltpu.VMEM((2, HEAD_DIM, t), F32),
                        pltpu.VMEM((2, 2, t, t), F32), pltpu.VMEM((2, 2, t, t), F32),
                        pltpu.VMEM((2, 2, t, t), BF16), pltpu.VMEM((2, 2, t, t), BF16),
                        pltpu.VMEM((2, 2, 1, t), F32)],
        compiler_params=_params(3),
        name="stick_attention",
    )(proj, proj, vt, jnp.asarray(tri, BF16))


def _dilated_body(slopes_ref, q0_ref, q1_ref, q2_ref, k0_ref, k1_ref, k2_ref, v_ref, o_ref):
    tq = C_TQ
    hp = pl.program_id(1)
    i = pl.program_id(2)
    seq = v_ref.shape[0]
    q_start = i * tq
    t_col = q_start + lax.broadcasted_iota(jnp.int32, (tq, 1), 0)
    q_refs = (q0_ref, q1_ref, q2_ref)
    k_refs = (k0_ref, k1_ref, k2_ref)
    for hh in range(2):
        outs, lses = [], []
        for g, (w, d) in enumerate(C_PATTERNS):
            span = min(w + tq, seq)
            k_start = pl.multiple_of(jnp.maximum(q_start + tq - span, 0), tq)
            qm = _half_masked(q_refs[g][...].astype(F32) * HEAD_DIM ** -0.5)[hh]
            kk = k_refs[g][pl.ds(k_start, span), :]
            vv = v_ref[pl.ds(k_start, span), hh * C_V_DIM:(hh + 1) * C_V_DIM]
            slope = slopes_ref[g * C_HEADS + hp * 2 + hh]
            dist = t_col - (k_start + lax.broadcasted_iota(jnp.int32, (1, span), 1))
            valid = (dist >= 0) & (dist <= w) & ((dist & (d - 1)) == 0)
            s = lax.dot_general(qm, kk, _NT, preferred_element_type=F32) - slope * dist.astype(F32)
            s = jnp.where(valid, s, NEG_INF)
            m = jnp.max(s, axis=1, keepdims=True)
            e = jnp.where(valid, jnp.exp(s - m), 0.0)
            l = jnp.sum(e, axis=1, keepdims=True)
            outs.append(jnp.dot(e.astype(BF16), vv, preferred_element_type=F32) / l)
            lses.append(m + jnp.log(l))
        mx = jnp.maximum(jnp.maximum(lses[0], lses[1]), lses[2])
        ws = [jnp.exp(x - mx) for x in lses]
        tot = ws[0] + ws[1] + ws[2]
        mixed = (ws[0] * outs[0] + ws[1] * outs[1] + ws[2] * outs[2]) / tot
        o_ref[:, hh * C_V_DIM:(hh + 1) * C_V_DIM] = mixed.astype(BF16)


def _dilated_attention(proj, batch, seq):
    tq = C_TQ
    nq = seq // tq
    zb = D_MODEL // LANES
    nqk = len(C_PATTERNS) * C_HEADS // 2
    vb = (D_MODEL + 2 * nqk * LANES) // (2 * C_V_DIM)
    q_spec = lambda g: pl.BlockSpec((tq, LANES), lambda b, h, i: (b * nq + i, zb + 2 * g + h))
    k_spec = lambda g: pl.BlockSpec((seq, LANES), lambda b, h, i: (b, zb + nqk + 2 * g + h))
    return pl.pallas_call(
        _dilated_body,
        grid=(batch, 2, nq),
        in_specs=[_smem_spec(), q_spec(0), q_spec(1), q_spec(2), k_spec(0), k_spec(1), k_spec(2),
                  pl.BlockSpec((seq, 2 * C_V_DIM), lambda b, h, i: (b, vb + h))],
        out_specs=pl.BlockSpec((tq, 2 * C_V_DIM), lambda b, h, i: (b * nq + i, h)),
        out_shape=jax.ShapeDtypeStruct((batch * seq, D_MODEL), BF16),
        compiler_params=_params(3),
        name="dilated_attention",
    )(jnp.asarray(_alibi_slopes(len(C_PATTERNS) * C_HEADS)), *([proj] * 7))


def _gelu_tanh(x):
    return 0.5 * x * (1.0 + jnp.tanh(math.sqrt(2.0 / math.pi) * (x + 0.044715 * x * x * x)))


def _compress_body(ak_ref, av_ref, pos_ref, w1_ref, w2_ref, o_ref):
    n = ak_ref.shape[2]
    acc = jnp.zeros((n, LANES), F32)
    for kv, a_ref in enumerate((ak_ref, av_ref)):
        a = a_ref[0, 0].astype(F32)
        first = jnp.dot((a + pos_ref[kv, 0]).astype(BF16), w1_ref[kv, 0], preferred_element_type=F32)
        second = jnp.dot((a + pos_ref[kv, 1]).astype(BF16), w1_ref[kv, 1], preferred_element_type=F32)
        hidden = _gelu_tanh(first + pltpu.roll(second, n - 1, 0))
        acc = acc + jnp.dot(hidden.astype(BF16), w2_ref[kv], preferred_element_type=F32)
    o_ref[0, 0] = acc.astype(BF16)


def _compress(ak, av, pos, w1, w2):
    batch, hk, n, width = ak.shape
    a_spec = pl.BlockSpec((1, 1, n, width), lambda b, h: (b, h, 0, 0))
    return pl.pallas_call(
        _compress_body,
        grid=(batch, hk),
        in_specs=[a_spec, a_spec,
                  pl.BlockSpec(pos.shape, lambda b, h: (0, 0, 0, 0)),
                  pl.BlockSpec(w1.shape, lambda b, h: (0, 0, 0, 0)),
                  pl.BlockSpec(w2.shape, lambda b, h: (0, 0, 0))],
        out_specs=pl.BlockSpec((1, 1, n, LANES), lambda b, h: (b, h, 0, 0)),
        out_shape=jax.ShapeDtypeStruct((batch, hk, n, LANES), BF16),
        compiler_params=_params(2),
        name="nsa_compress",
    )(ak, av, pos, w1, w2)


def _nsa_body(slopes_ref, q_ref, sel_ref, vst_ref, win_ref, vwt_ref, cmp_ref, vct_ref, ovt_ref, kf_ref,
              oc_ref, os_ref, ow_ref, m_sc, acc_sc, qt_sc, neg_sc, u_sc, p_sc, a_sc):
    t, g_n, dh = A_T, A_GROUP, HEAD_DIM
    hk = pl.program_id(1)
    i = pl.program_id(2)
    q_start = i * t
    slopes2 = [slopes_ref[hk * g_n + g] * LOG2E for g in range(g_n)]

    lane = _lane_iota((1, LANES))
    t_lane = q_start + _lane_iota((1, t))

    q32 = q_ref[...].astype(F32) * (dh ** -0.5 * LOG2E)
    qpad = []
    for g in range(g_n):
        blk = q32[:, (g // 2) * LANES:(g // 2 + 1) * LANES]
        if g % 2:
            blk = pltpu.roll(blk, dh, 1)
        qpad.append(jnp.where(lane < dh, blk, 0.0).astype(BF16))

    qt = q32.T
    feat_row = _row_iota((A_FEAT_ROWS // 2, t))
    slope_rows = []
    for g in range(g_n):
        sr = jnp.zeros((A_FEAT_ROWS // 2, t), F32) + slopes2[g]
        hi = sr.astype(BF16).astype(F32)
        lo = (sr - hi).astype(BF16).astype(F32)
        slope_rows.append(jnp.where(feat_row == 0, hi, jnp.where(feat_row == 1, lo, 0.0)))
        for par in range(2):
            qt_sc[par, g, :dh] = qt[g * dh:(g + 1) * dh].astype(BF16)
            qt_sc[par, g, dh + A_FEAT_ROWS:] = jnp.zeros((LANES - dh - A_FEAT_ROWS, t), BF16)

    def store_token_major(ref, per_head):
        ref[...] = jnp.concatenate(per_head, axis=0).T.astype(BF16)

    n_cmp = cmp_ref.shape[2]
    ckv = cmp_ref[0, 0]
    vct = vct_ref[0, 0]
    cmp_end = A_CMP_STRIDE * _row_iota((n_cmp, t)) + (A_CMP_BLOCK - 1)
    cvalid = t_lane >= cmp_end
    cpos = (cmp_end - q_start).astype(F32)
    raw = [lax.dot_general(ckv, qpad[g], _NT, preferred_element_type=F32) for g in range(g_n)]
    probs = []
    for g in range(g_n):
        u = jnp.where(cvalid, raw[g] + slopes2[g] * cpos, NEG_INF)
        e = jnp.where(cvalid, jnp.exp2(u - jnp.max(u, axis=0, keepdims=True)), 0.0)
        probs.append(e * (1.0 / jnp.maximum(jnp.sum(e, axis=0, keepdims=True), TINY)))
    store_token_major(oc_ref, [jnp.dot(vct, p.astype(BF16), preferred_element_type=F32) for p in probs])
    p_sum = (probs[0] + probs[1]) + (probs[2] + probs[3])

    n_sel = ovt_ref.shape[0]
    hi, lo = _split_bf16(p_sum)
    ovt = ovt_ref[...]
    imp = jnp.dot(ovt, hi, preferred_element_type=F32) + jnp.dot(ovt, lo, preferred_element_type=F32)
    jrow = _row_iota((n_sel, t))
    cur = jnp.right_shift(t_lane, A_SEL_SHIFT)
    forced = (jrow == 0) | (jrow == cur) | (jrow == cur - 1)
    imp = jnp.where(jrow > cur, -1.0, imp + jnp.where(forced, A_FORCE_BONUS, 0.0))
    jrow_f = jrow.astype(F32)
    selected = jnp.zeros((n_sel, t), F32)
    for _ in range(min(A_SEL_TOPK, n_sel)):
        top = jnp.max(imp, axis=0, keepdims=True)
        first = jnp.min(jnp.where(imp == top, jrow_f, float(n_sel)), axis=0, keepdims=True)
        pick = jrow_f == first
        selected = jnp.where(pick, 1.0, selected)
        imp = jnp.where(pick, -2.0, imp)
    neg_sc[...] = jnp.where(selected > 0.5, 0.0, NEG_INF)

    ones = jnp.ones((ONES_ROWS, t), BF16)
    blocks_per_pair = 2 * t // A_SEL_BLOCK

    def branch(kv_ref, vt_ref, block_masked, keep_fn, last, out_ref):
        m_sc[...] = jnp.full(m_sc.shape, NEG_INF, F32)
        acc_sc[...] = jnp.zeros(acc_sc.shape, F32)

        def set_features(par, mask_rows):
            for g in range(g_n):
                qt_sc[par, g, dh:dh + A_FEAT_ROWS] = jnp.concatenate([slope_rows[g], mask_rows], axis=0).astype(BF16)

        if not block_masked:
            for par in range(2):
                set_features(par, jnp.zeros((A_FEAT_ROWS // 2, t), F32))

        def score(s, par):
            tile = i - s
            kv = kv_ref[pl.ds(pl.multiple_of(tile * t, t), t), :]
            k_aug = jnp.where(lane < dh, kv, kf_ref[tile % 2])
            if block_masked:
                pair_start = pl.multiple_of((tile // 2) * blocks_per_pair, blocks_per_pair)
                set_features(par, neg_sc[pl.ds(pair_start, blocks_per_pair), :])
            for g in range(g_n):
                u_sc[par, g] = jnp.dot(k_aug, qt_sc[par, g], preferred_element_type=F32)

        def softmax(s, par, diagonal):
            keep = keep_fn(s, diagonal)
            offset = jnp.asarray(-s * t, jnp.int32).astype(F32)
            for g in range(g_n):
                cj = slopes2[g] * offset
                u = u_sc[par, g]
                if keep is not None:
                    u = jnp.where(keep, u, NEG_INF)
                m_old = m_sc[g]
                m_new = jnp.maximum(m_old, jnp.max(u, axis=0, keepdims=True) + cj)
                p_sc[par, g] = jnp.exp2(u - (m_new - cj)).astype(BF16)
                a_sc[par, g] = jnp.exp2(m_old - m_new)
                m_sc[g] = m_new

        def accumulate(s, par):
            vaug = jnp.concatenate([vt_ref[0, 0, i - s], ones], axis=0)
            pv = [jnp.dot(vaug, p_sc[par, g], preferred_element_type=F32) for g in range(g_n)]
            for g in range(g_n):
                acc_sc[g] = a_sc[par, g] * acc_sc[g] + pv[g]

        _pipeline3(last, score, softmax, accumulate)
        store_token_major(out_ref, [acc_sc[g, :dh] / jnp.maximum(acc_sc[g, dh:dh + 1], TINY)
                                    for g in range(g_n)])

    def keep_selected(s, diagonal):
        return _row_iota((t, t)) <= _lane_iota((t, t)) if diagonal else None

    def keep_window(s, diagonal):
        rel = _row_iota((t, t)) - _lane_iota((t, t))
        return rel <= 0 if diagonal else rel > s * t - A_WINDOW

    branch(sel_ref, vst_ref, True, keep_selected, i, os_ref)
    branch(win_ref, vwt_ref, False, keep_window, jnp.minimum(i, A_WINDOW // t), ow_ref)


def _nsa_attention(proj, cmp_kv, batch, seq):
    t = A_T
    nq = seq // t
    hk_n, dh, g_n = A_KV_HEADS, HEAD_DIM, A_GROUP
    n_cmp = cmp_kv.shape[2]
    n_sel = seq // A_SEL_BLOCK
    ov = np.zeros((n_cmp, n_sel), np.float32)
    real = (seq - A_CMP_BLOCK) // A_CMP_STRIDE + 1
    cidx = A_CMP_STRIDE * np.arange(real)[:, None] + np.arange(A_CMP_BLOCK)[None, :]
    np.add.at(ov, (np.repeat(np.arange(real), A_CMP_BLOCK), (cidx // A_SEL_BLOCK).ravel()), 1.0 / A_CMP_BLOCK)
    qb = D_MODEL // (g_n * dh)
    sb = 2 * D_MODEL // LANES
    wb = sb + hk_n
    c_sel, c_win = 2 * D_MODEL, 2 * D_MODEL + 2 * hk_n * dh

    def values_transposed(c0):
        v = proj[:, c0:c0 + 2 * hk_n * dh].reshape(-1, hk_n, 2, dh)[:, :, 1, :]
        return _tile_transposed(v.reshape(-1, hk_n * dh), batch, hk_n, dh, t)

    vct = cmp_kv[..., dh:].transpose(0, 1, 3, 2)
    kf = np.zeros((2, t, LANES), np.float32)
    kf[:, :, dh] = kf[:, :, dh + 1] = np.arange(t)
    for par in range(2):
        blk = par * (t // A_SEL_BLOCK) + np.arange(t) // A_SEL_BLOCK
        kf[par, np.arange(t), dh + A_FEAT_ROWS // 2 + blk] = 1.0
    vt_spec = pl.BlockSpec((1, 1, nq, dh, t), lambda b, h, i: (b, h, 0, 0, 0))
    o_spec = pl.BlockSpec((t, g_n * dh), lambda b, h, i: (b * nq + i, h))
    o_shape = jax.ShapeDtypeStruct((batch * seq, D_MODEL), BF16)
    return pl.pallas_call(
        _nsa_body,
        grid=(batch, hk_n, nq),
        in_specs=[_smem_spec(),
                  pl.BlockSpec((t, g_n * dh), lambda b, h, i: (b * nq + i, qb + h)),
                  pl.BlockSpec((seq, LANES), lambda b, h, i: (b, sb + h)),
                  vt_spec,
                  pl.BlockSpec((seq, LANES), lambda b, h, i: (b, wb + h)),
                  vt_spec,
                  pl.BlockSpec((1, 1, n_cmp, LANES), lambda b, h, i: (b, h, 0, 0)),
                  pl.BlockSpec((1, 1, dh, n_cmp), lambda b, h, i: (b, h, 0, 0)),
                  pl.BlockSpec((n_sel, n_cmp), lambda b, h, i: (0, 0)),
                  pl.BlockSpec((2, t, LANES), lambda b, h, i: (0, 0, 0))],
        out_specs=[o_spec, o_spec, o_spec],
        out_shape=[o_shape, o_shape, o_shape],
        scratch_shapes=[pltpu.VMEM((g_n, 1, t), F32),
                        pltpu.VMEM((g_n, dh + ONES_ROWS, t), F32),
                        pltpu.VMEM((2, g_n, LANES, t), BF16),
                        pltpu.VMEM((n_sel, t), F32),
                        pltpu.VMEM((2, g_n, t, t), F32),
                        pltpu.VMEM((2, g_n, t, t), BF16),
                        pltpu.VMEM((2, g_n, 1, t), F32)],
        compiler_params=_params(3),
        name="nsa_attention",
    )(jnp.asarray(_alibi_slopes(A_HEADS)), proj, proj, values_transposed(c_sel), proj,
      values_transposed(c_win), cmp_kv, vct, jnp.asarray(ov.T, BF16), jnp.asarray(kf, BF16))


def _nsa_layer(h, batch, seq, gain_pre, gain_post, w_in, w_out, pos_k, pos_v, w1_k, w2_k, w1_v, w2_v):
    dm, dh, hk_n = D_MODEL, HEAD_DIM, A_KV_HEADS
    kvw = hk_n * dh
    q0, kc0, vc0, ks0, vs0, kw0, vw0, gl0, z0 = np.cumsum((0, dm) + (kvw,) * 6 + (3 * A_HEADS,))
    cols = [np.arange(z0, z0 + dm), np.arange(q0, q0 + dm)]
    for k0, v0 in ((ks0, vs0), (kw0, vw0)):
        for hh in range(hk_n):
            cols += [np.arange(k0 + hh * dh, k0 + (hh + 1) * dh), np.arange(v0 + hh * dh, v0 + (hh + 1) * dh)]
    cols += [np.arange(kc0, kc0 + kvw), np.arange(vc0, vc0 + kvw), np.arange(gl0, gl0 + 3 * A_HEADS)]
    cols = np.concatenate(cols)
    w = jnp.pad(w_in[:, cols], ((0, 0), (0, LANES - 3 * A_HEADS))).astype(BF16)
    proj = _proj_in(h, gain_pre, w)

    grp = A_CMP_STRIDE
    cmp_col = 2 * dm + 4 * kvw

    def groups(c0):
        a = proj[:, c0:c0 + kvw].reshape(batch, seq // grp, grp, hk_n, dh)
        return a.transpose(0, 3, 1, 2, 4).reshape(batch, hk_n, seq // grp, grp * dh)

    half = grp * dh
    pos = jnp.stack([pos_k.reshape(2, 1, half), pos_v.reshape(2, 1, half)])
    w1 = jnp.stack([w1_k.reshape(2, half, A_CMP_HIDDEN), w1_v.reshape(2, half, A_CMP_HIDDEN)]).astype(BF16)
    w2 = jnp.stack([jnp.pad(w2_k, ((0, 0), (0, dh))), jnp.pad(w2_v, ((0, 0), (dh, 0)))]).astype(BF16)
    cmp_kv = _compress(groups(cmp_col), groups(cmp_col + kvw), pos, w1, w2)

    oc, osel, ow = _nsa_attention(proj, cmp_kv, batch, seq)
    gl_blk = (cmp_col + 2 * kvw) // LANES
    return _proj_out([oc, osel, ow], proj, 0, w_out.astype(BF16), gain_post, h, gl_blk=gl_blk)


def kernel(x, norm_pre, norm_post, a_w_in, a_w_out, a_cmp_pos_k, a_cmp_pos_v, a_cmp_w1_k, a_cmp_w2_k,
           a_cmp_w1_v, a_cmp_w2_v, b_w_in, b_w_out, b_lambda, b_sub_gain, c_w_in, c_w_out, d_w_in, d_w_out):
    batch, seq, dm = x.shape
    h = x.reshape(batch * seq, dm)
    for i in range(DEPTH):
        mixer, j = i % 4, i // 4
        if mixer == 0:
            h = _nsa_layer(h, batch, seq, norm_pre[i], norm_post[i], a_w_in[j], a_w_out[j],
                           a_cmp_pos_k[j], a_cmp_pos_v[j], a_cmp_w1_k[j], a_cmp_w2_k[j],
                           a_cmp_w1_v[j], a_cmp_w2_v[j])
            continue
        if mixer == 1:
            proj = _proj_in(h, norm_pre[i], b_w_in[j].astype(BF16))
            lambda_init = 0.8 - 0.6 * math.exp(-0.3 * i)
            o = _diff_attention(proj, b_lambda[j], b_sub_gain[j], lambda_init, batch, seq)
            z_blk, w_out = 3, b_w_out[j]
        elif mixer == 2:
            qk = 2 * len(C_PATTERNS) * C_HEADS * HEAD_DIM
            w = jnp.concatenate([c_w_in[j][:, qk + dm:], c_w_in[j][:, :qk + dm]], axis=1)
            proj = _proj_in(h, norm_pre[i], w.astype(BF16))
            o = _dilated_attention(proj, batch, seq)
            z_blk, w_out = 0, c_w_out[j]
        else:
            proj = _proj_in(h, norm_pre[i], d_w_in[j].astype(BF16))
            o = _stick_attention(proj, batch, seq)
            z_blk, w_out = 3, d_w_out[j]
        h = _proj_out([o], proj, z_blk, w_out.astype(BF16), norm_post[i], h)
    return h.reshape(batch, seq, dm)
```

```python
import functools
import math

import jax
import jax.numpy as jnp
import numpy as np
from jax import lax
from jax.experimental import pallas as pl
from jax.experimental.pallas import tpu as pltpu

F32 = jnp.float32
BF16 = jnp.bfloat16

D_MODEL = 1024
HEAD_DIM = 64
DEPTH = 4
RMS_EPS = 1e-6
NEG_INF = -1e30
TINY = 1e-30
LANES = 128
ONES_ROWS = 16
LOG2E = math.log2(math.e)
VMEM_LIMIT = 48 * 1024 * 1024

A_HEADS = 16
A_KV_HEADS = 4
A_GROUP = 4
A_CMP_BLOCK = 32
A_CMP_STRIDE = 16
A_CMP_HIDDEN = 256
A_SEL_BLOCK = 64
A_SEL_SHIFT = 6
A_SEL_TOPK = 16
A_WINDOW = 512
A_FORCE_BONUS = 1e3
A_T = 256
A_FEAT_ROWS = 16
B_HEADS = 8
B_T = 256
C_PATTERNS = ((128, 1), (512, 4), (2048, 16))
C_HEADS = 4
C_V_DIM = 256
C_T = 256
D_HEADS = 16
D_T = 256

PROJ_TM = 512
VT_TILE = 256
PROJ_CHUNK = 512

_NT = (((1,), (1,)), ((), ()))


def _alibi_slopes(n):
    return np.array([2.0 ** (-8.0 * (i + 1) / n) for i in range(n)], np.float32)


def _params(n_grid):
    return pltpu.CompilerParams(dimension_semantics=("arbitrary",) * n_grid,
                                vmem_limit_bytes=VMEM_LIMIT)


def _split_bf16(x):
    hi = x.astype(BF16)
    lo = (x - hi.astype(F32)).astype(BF16)
    return hi, lo


def _sigmoid(x):
    return 1.0 / (1.0 + jnp.exp(-x))


def _proj_in_body(h_ref, g_ref, w_ref, o_ref, *vt_refs, plans):
    x = h_ref[...]
    ms = jnp.mean(x * x, axis=-1, keepdims=True)
    u = (x * lax.rsqrt(ms + RMS_EPS) * g_ref[...]).astype(BF16)
    n = o_ref.shape[1]
    for c in range(0, n, PROJ_CHUNK):
        e = min(c + PROJ_CHUNK, n)
        y = jnp.dot(u, w_ref[:, c:e], preferred_element_type=F32)
        o_ref[:, c:e] = y.astype(BF16)
        for vt_ref, heads in zip(vt_refs, plans):
            for hd, (col, width, row, rows) in enumerate(heads):
                if c <= col and col + width <= e:
                    for r in range(PROJ_TM // VT_TILE):
                        tile = y[r * VT_TILE:(r + 1) * VT_TILE, col - c:col - c + width].T
                        vt_ref[0, hd, r] = tile[row:row + rows].astype(BF16)


def _proj_in(h, gain, w, batch, plans=()):
    m, n = h.shape[0], w.shape[1]
    steps_per_batch = m // batch // PROJ_TM
    vt_shapes = [jax.ShapeDtypeStruct((batch, len(p), m // batch // VT_TILE, p[0][3], VT_TILE), BF16) for p in plans]
    vt_specs = [pl.BlockSpec((1, len(p), PROJ_TM // VT_TILE, p[0][3], VT_TILE),
                             lambda i: (i // steps_per_batch, 0, i % steps_per_batch, 0, 0)) for p in plans]
    outs = pl.pallas_call(
        functools.partial(_proj_in_body, plans=tuple(plans)),
        grid=(m // PROJ_TM,),
        in_specs=[pl.BlockSpec((PROJ_TM, D_MODEL), lambda i: (i, 0)),
                  pl.BlockSpec((1, D_MODEL), lambda i: (0, 0)),
                  pl.BlockSpec((D_MODEL, n), lambda i: (0, 0))],
        out_specs=[pl.BlockSpec((PROJ_TM, n), lambda i: (i, 0))] + vt_specs,
        out_shape=[jax.ShapeDtypeStruct((m, n), BF16)] + vt_shapes,
        compiler_params=_params(1),
        name="proj_in",
    )(h, gain.reshape(1, D_MODEL), w)
    return outs


def _finish(o, z_ref, w_ref, g_ref, h_ref, out_ref):
    z = z_ref[...].astype(F32)
    gated = (o * (z * _sigmoid(z))).astype(BF16)
    y = jnp.dot(gated, w_ref[...], preferred_element_type=F32)
    ms = jnp.mean(y * y, axis=-1, keepdims=True)
    out_ref[...] = h_ref[...] + y * lax.rsqrt(ms + RMS_EPS) * g_ref[...]


def _proj_out_body(o_ref, z_ref, w_ref, g_ref, h_ref, out_ref):
    _finish(o_ref[...].astype(F32), z_ref, w_ref, g_ref, h_ref, out_ref)


def _proj_out_nsa_body(oc_ref, os_ref, ow_ref, gl_ref, eg_ref, z_ref, w_ref, g_ref, h_ref, out_ref):
    hi, lo = _split_bf16(_sigmoid(gl_ref[...].astype(F32)))
    eg = eg_ref[...]
    gates = jnp.dot(hi, eg, preferred_element_type=F32) + jnp.dot(lo, eg, preferred_element_type=F32)
    o = (gates[:, :D_MODEL] * oc_ref[...].astype(F32)
         + gates[:, D_MODEL:2 * D_MODEL] * os_ref[...].astype(F32)
         + gates[:, 2 * D_MODEL:] * ow_ref[...].astype(F32))
    _finish(o, z_ref, w_ref, g_ref, h_ref, out_ref)


def _proj_out(o_list, proj, z_blk, w_out, gain, h, gl_blk=None):
    m = h.shape[0]
    tm = PROJ_TM
    row = lambda i: (i, 0)
    fixed = lambda i: (0, 0)
    o_specs = [pl.BlockSpec((tm, D_MODEL), row) for _ in o_list]
    tail_specs = [pl.BlockSpec((tm, D_MODEL), lambda i: (i, z_blk)),
                  pl.BlockSpec((D_MODEL, D_MODEL), fixed),
                  pl.BlockSpec((1, D_MODEL), fixed),
                  pl.BlockSpec((tm, D_MODEL), row)]
    tail = [proj, w_out, gain.reshape(1, D_MODEL), h]
    if gl_blk is None:
        body, in_specs, args = _proj_out_body, o_specs + tail_specs, list(o_list) + tail
    else:
        eg = np.zeros((LANES, 3 * D_MODEL), np.float32)
        for br in range(3):
            for hd in range(A_HEADS):
                c0 = br * D_MODEL + hd * HEAD_DIM
                eg[br * A_HEADS + hd, c0:c0 + HEAD_DIM] = 1.0
        body = _proj_out_nsa_body
        in_specs = o_specs + [pl.BlockSpec((tm, LANES), lambda i: (i, gl_blk)),
                              pl.BlockSpec((LANES, 3 * D_MODEL), fixed)] + tail_specs
        args = list(o_list) + [proj, jnp.asarray(eg, BF16)] + tail
    return pl.pallas_call(
        body,
        grid=(m // tm,),
        in_specs=in_specs,
        out_specs=pl.BlockSpec((tm, D_MODEL), row),
        out_shape=jax.ShapeDtypeStruct((m, D_MODEL), F32),
        compiler_params=_params(1),
        name="proj_out",
    )(*args)


def _half_masked(q32):
    lane = lax.broadcasted_iota(jnp.int32, (1, LANES), 1)
    lo = jnp.where(lane < HEAD_DIM, q32, 0.0).astype(BF16)
    hi = jnp.where(lane >= HEAD_DIM, q32, 0.0).astype(BF16)
    return lo, hi


def _smem_spec():
    return pl.BlockSpec(memory_space=pltpu.SMEM)


def _row_iota(shape):
    return lax.broadcasted_iota(jnp.int32, shape, 0)


def _lane_iota(shape):
    return lax.broadcasted_iota(jnp.int32, shape, 1)


def _pipeline3(i, score, softmax, accumulate):
    score(0, 0)
    score(jnp.minimum(1, i), 1)
    softmax(0, 0, True)

    def pair(s):
        accumulate(s - 1, 0)
        score(s + 1, 0)
        softmax(s, 1, False)
        accumulate(s, 1)
        score(jnp.minimum(s + 2, i), 1)
        softmax(s + 1, 0, False)

    def quad_body(r, carry):
        pair(4 * r + 1)
        pair(4 * r + 3)
        return carry

    def pair_body(r, carry):
        pair(4 * (i // 4) + 2 * r + 1)
        return carry

    lax.fori_loop(0, i // 4, quad_body, 0)
    lax.fori_loop(0, (i % 4) // 2, pair_body, 0)

    @pl.when(i % 2 == 1)
    def _():
        accumulate(i - 1, 0)
        softmax(i, 1, False)
        accumulate(i, 1)

    @pl.when(i % 2 == 0)
    def _():
        accumulate(i, 0)


def _diff_body(slopes_ref, q_ref, k_ref, vt_ref, lam_ref, sg_ref, o_ref,
               m_sc, acc_sc, b0_sc, u_sc, p_sc, a_sc, *, lambda_init):
    t = B_T
    dv = 2 * HEAD_DIM
    hd = pl.program_id(1)
    i = pl.program_id(2)
    slope2 = slopes_ref[hd] * LOG2E

    @pl.when(i == 0)
    def _():
        b0_sc[...] = slope2 * _row_iota((t, t)).astype(F32)

    qm = _half_masked(q_ref[...].astype(F32) * (HEAD_DIM ** -0.5 * LOG2E))
    m_sc[...] = jnp.full(m_sc.shape, NEG_INF, F32)
    acc_sc[...] = jnp.zeros(acc_sc.shape, F32)
    ones = jnp.ones((ONES_ROWS, t), BF16)

    def score(s, par):
        k = k_ref[pl.ds(pl.multiple_of((i - s) * t, t), t), :]
        for mp in range(2):
            u_sc[par, mp] = lax.dot_general(k, qm[mp], _NT, preferred_element_type=F32)

    def softmax(s, par, diagonal):
        cj = slope2 * jnp.asarray(-s * t, jnp.int32).astype(F32)
        for mp in range(2):
            u = u_sc[par, mp] + b0_sc[...]
            if diagonal:
                u = jnp.where(_row_iota((t, t)) <= _lane_iota((t, t)), u, NEG_INF)
            m_old = m_sc[mp]
            m_new = jnp.maximum(m_old, jnp.max(u, axis=0, keepdims=True) + cj)
            p_sc[par, mp] = jnp.exp2(u - (m_new - cj)).astype(BF16)
            a_sc[par, mp] = jnp.exp2(m_old - m_new)
            m_sc[mp] = m_new

    def accumulate(s, par):
        vaug = jnp.concatenate([vt_ref[0, 0, i - s], ones], axis=0)
        pv = [jnp.dot(vaug, p_sc[par, mp], preferred_element_type=F32) for mp in range(2)]
        for mp in range(2):
            acc_sc[mp] = a_sc[par, mp] * acc_sc[mp] + pv[mp]

    _pipeline3(i, score, softmax, accumulate)

    lam = lam_ref[...]
    lam_full = (jnp.exp(jnp.sum(lam[0:1] * lam[1:2], axis=1, keepdims=True))
                - jnp.exp(jnp.sum(lam[2:3] * lam[3:4], axis=1, keepdims=True)) + lambda_init)
    o0 = acc_sc[0, :dv] / jnp.maximum(acc_sc[0, dv:dv + 1], TINY)
    o1 = acc_sc[1, :dv] / jnp.maximum(acc_sc[1, dv:dv + 1], TINY)
    a = o0 - lam_full * o1
    ms = jnp.mean(a * a, axis=0, keepdims=True)
    y = (a * lax.rsqrt(ms + RMS_EPS) * sg_ref[...]) * (1.0 - lambda_init)
    o_ref[...] = y.T.astype(BF16)


def _diff_attention(proj, vt, lam, sub_gain, lambda_init, batch, seq):
    t = B_T
    nq = seq // t
    nh = B_HEADS
    dv = 2 * HEAD_DIM
    return pl.pallas_call(
        functools.partial(_diff_body, lambda_init=lambda_init),
        grid=(batch, nh, nq),
        in_specs=[_smem_spec(),
                  pl.BlockSpec((t, LANES), lambda b, h, i: (b * nq + i, h)),
                  pl.BlockSpec((seq, LANES), lambda b, h, i: (b, nh + h)),
                  pl.BlockSpec((1, 1, nq, dv, t), lambda b, h, i: (b, h, 0, 0, 0)),
                  pl.BlockSpec((4, HEAD_DIM), lambda b, h, i: (0, 0)),
                  pl.BlockSpec((dv, 1), lambda b, h, i: (0, 0))],
        out_specs=pl.BlockSpec((t, LANES), lambda b, h, i: (b * nq + i, h)),
        out_shape=jax.ShapeDtypeStruct((batch * seq, D_MODEL), BF16),
        scratch_shapes=[pltpu.VMEM((2, 1, t), F32), pltpu.VMEM((2, dv + ONES_ROWS, t), F32),
                        pltpu.VMEM((t, t), F32),
                        pltpu.VMEM((2, 2, t, t), F32), pltpu.VMEM((2, 2, t, t), BF16),
                        pltpu.VMEM((2, 2, 1, t), F32)],
        compiler_params=_params(3),
        name="diff_attention",
    )(jnp.asarray(_alibi_slopes(nh)), proj, proj, vt, lam, sub_gain.reshape(dv, 1))


def _softplus2(x):
    sign = jnp.uint32(0x80000000)
    neg_abs = lax.bitcast_convert_type(lax.bitcast_convert_type(x, jnp.uint32) | sign, F32)
    return jnp.maximum(x, 0.0) + jnp.log(1.0 + jnp.exp2(neg_abs)) * LOG2E


def _stick_body(q_ref, k_ref, vt_ref, tri_ref, o_ref, c_sc, acc_sc, u_sc, ls_sc, hl_sc, a_sc, f_sc):
    t = D_T
    dh = HEAD_DIM
    i = pl.program_id(2)
    qm = _half_masked(q_ref[...].astype(F32) * (dh ** -0.5 * LOG2E))
    c_sc[...] = jnp.zeros(c_sc.shape, F32)
    acc_sc[...] = jnp.zeros(acc_sc.shape, F32)
    a_sc[...] = jnp.zeros(a_sc.shape, BF16)
    f_sc[...] = jnp.zeros(f_sc.shape, F32)

    def logits(s, par):
        k = k_ref[pl.ds(pl.multiple_of((i - s) * t, t), t), :]
        for hh in range(2):
            u_sc[par, hh] = lax.dot_general(k, qm[hh], _NT, preferred_element_type=F32)

    def softplus(par, diagonal):
        for hh in range(2):
            logit = u_sc[par, hh]
            sp = _softplus2(logit)
            log_sig = logit - sp
            if diagonal:
                before = _row_iota((t, t)) < _lane_iota((t, t))
                sp = jnp.where(before, sp, 0.0)
                log_sig = jnp.where(before, log_sig, NEG_INF)
            ls_sc[par, hh] = log_sig
            hl_sc[par, hh] = sp.astype(BF16)

    def suffix_sums(par):
        tri = tri_ref[...]
        return [jnp.dot(tri, hl_sc[par, hh], preferred_element_type=F32) for hh in range(2)]

    def weights(sums, par):
        for hh in range(2):
            a_sc[par, hh] = jnp.exp2(ls_sc[par, hh] + sums[hh][:t]).astype(BF16)
            f_sc[par, hh] = jnp.exp2(c_sc[hh])
            c_sc[hh] = c_sc[hh] + sums[hh][t:t + 1]

    def values(s, par):
        vt = vt_ref[0, 0, jnp.minimum(i - s, i)]
        for hh in range(2):
            acc_sc[hh] = acc_sc[hh] + f_sc[par, hh] * jnp.dot(vt[hh * dh:(hh + 1) * dh], a_sc[par, hh],
                                                              preferred_element_type=F32)

    def iteration(s, par, prefetch):
        sums = suffix_sums(1 - par)
        values(s - 2, par)
        if prefetch:
            logits(jnp.minimum(s + 1, i), 1 - par)
        softplus(par, False)
        weights(sums, 1 - par)

    def drain(par):
        sums = suffix_sums(par)
        values(i - 1, 1 - par)
        weights(sums, par)
        values(i, par)

    logits(0, 0)
    logits(jnp.minimum(1, i), 1)
    softplus(0, True)

    def pair(s):
        iteration(s, 1, True)
        iteration(s + 1, 0, True)

    def quad_body(r, carry):
        pair(4 * r + 1)
        pair(4 * r + 3)
        return carry

    def pair_body(r, carry):
        pair(4 * (i // 4) + 2 * r + 1)
        return carry

    lax.fori_loop(0, i // 4, quad_body, 0)
    lax.fori_loop(0, (i % 4) // 2, pair_body, 0)

    @pl.when(i % 2 == 1)
    def _():
        iteration(i, 1, False)
        drain(1)

    @pl.when(i % 2 == 0)
    def _():
        drain(0)

    o_ref[...] = jnp.concatenate([acc_sc[0], acc_sc[1]], axis=0).T.astype(BF16)


def _stick_attention(proj, vt, batch, seq):
    t = D_T
    nq = seq // t
    nb = D_HEADS // 2
    tri = -np.concatenate([np.triu(np.ones((t, t), np.float32), 1),
                           np.ones((ONES_ROWS, t), np.float32)])
    return pl.pallas_call(
        _stick_body,
        grid=(batch, nb, nq),
        in_specs=[pl.BlockSpec((t, LANES), lambda b, h, i: (b * nq + i, h)),
                  pl.BlockSpec((seq, LANES), lambda b, h, i: (b, nb + h)),
                  pl.BlockSpec((1, 1, nq, LANES, t), lambda b, h, i: (b, h, 0, 0, 0)),
                  pl.BlockSpec((t + ONES_ROWS, t), lambda b, h, i: (0, 0))],
        out_specs=pl.BlockSpec((t, LANES), lambda b, h, i: (b * nq + i, h)),
        out_shape=jax.ShapeDtypeStruct((batch * seq, D_MODEL), BF16),
        scratch_shapes=[pltpu.VMEM((2, 1, t), F32), pltpu.VMEM((2, HEAD_DIM, t), F32),
                        pltpu.VMEM((2, 2, t, t), F32), pltpu.VMEM((2, 2, t, t), F32),
                        pltpu.VMEM((2, 2, t, t), BF16), pltpu.VMEM((2, 2, t, t), BF16),
                        pltpu.VMEM((2, 2, 1, t), F32)],
        compiler_params=_params(3),
        name="stick_attention",
    )(proj, proj, vt, jnp.asarray(tri, BF16))


def _dilated_body(slopes_ref, q0_ref, q1_ref, q2_ref, k0_ref, k1_ref, k2_ref, vt_ref, o_ref,
                  bias0_sc, bias1_sc, bias2_sc):
    tq, dv = C_T, C_V_DIM
    hp = pl.program_id(1)
    i = pl.program_id(2)
    q_start = i * tq
    q_refs = (q0_ref, q1_ref, q2_ref)
    k_refs = (k0_ref, k1_ref, k2_ref)
    bias_scs = (bias0_sc, bias1_sc, bias2_sc)
    seq = k0_ref.shape[0]
    spans = [min(-(-(w + tq) // tq) * tq, seq) for w, _ in C_PATTERNS]
    slopes2 = [[slopes_ref[g * C_HEADS + hp * 2 + hh] * LOG2E for g in range(len(C_PATTERNS))]
               for hh in range(2)]

    for g, (w, d) in enumerate(C_PATTERNS):
        @pl.when(i < spans[g] // tq)
        def _(g=g, w=w, d=d):
            span = spans[g]
            dist = q_start + _lane_iota((span, tq)) - _row_iota((span, tq))
            valid = (dist >= 0) & (dist <= w) & ((dist & (d - 1)) == 0)
            dist_f = dist.astype(F32)
            for hh in range(2):
                bias_scs[g][hh] = jnp.where(valid, -slopes2[hh][g] * dist_f, NEG_INF)

    ones = jnp.ones((ONES_ROWS, tq), BF16)
    for hh in range(2):
        qms = [_half_masked(q_refs[g][...].astype(F32) * (HEAD_DIM ** -0.5 * LOG2E))[hh]
               for g in range(len(C_PATTERNS))]
        starts = [jnp.maximum(q_start + tq - spans[g], 0) for g in range(len(C_PATTERNS))]
        raws = [lax.dot_general(k_refs[g][pl.ds(pl.multiple_of(starts[g], tq), spans[g]), :], qms[g], _NT,
                                preferred_element_type=F32) for g in range(len(C_PATTERNS))]
        outs, lses = [], []
        for g in range(len(C_PATTERNS)):
            span = spans[g]
            u = raws[g] + bias_scs[g][hh]
            m = jnp.max(u, axis=0, keepdims=True)
            e = jnp.exp2(u - m).astype(BF16)
            first_tile = starts[g] // tq
            acc = jnp.zeros((dv + ONES_ROWS, tq), F32)
            for r in range(span // tq):
                vaug = jnp.concatenate([vt_ref[0, hh, first_tile + r], ones], axis=0)
                acc = acc + jnp.dot(vaug, e[r * tq:(r + 1) * tq], preferred_element_type=F32)
            l = acc[dv:dv + 1]
            outs.append(acc[:dv] / l)
            lses.append(m + jnp.log(l) * LOG2E)
        mx = jnp.maximum(jnp.maximum(lses[0], lses[1]), lses[2])
        ws = [jnp.exp2(x - mx) for x in lses]
        mixed = (ws[0] * outs[0] + ws[1] * outs[1] + ws[2] * outs[2]) / (ws[0] + ws[1] + ws[2])
        o_ref[:, hh * dv:(hh + 1) * dv] = mixed.T.astype(BF16)


def _dilated_attention(proj, vt, batch, seq):
    tq = C_T
    nq = seq // tq
    zb = D_MODEL // LANES
    nqk = len(C_PATTERNS) * C_HEADS // 2
    spans = [min(-(-(w + tq) // tq) * tq, seq) for w, _ in C_PATTERNS]
    q_spec = lambda g: pl.BlockSpec((tq, LANES), lambda b, h, i: (b * nq + i, zb + 2 * g + h))
    k_spec = lambda g: pl.BlockSpec((seq, LANES), lambda b, h, i: (b, zb + nqk + 2 * g + h))
    return pl.pallas_call(
        _dilated_body,
        grid=(batch, 2, nq),
        in_specs=[_smem_spec(), q_spec(0), q_spec(1), q_spec(2), k_spec(0), k_spec(1), k_spec(2),
                  pl.BlockSpec((1, 2, nq, C_V_DIM, tq), lambda b, h, i: (b, h, 0, 0, 0))],
        out_specs=pl.BlockSpec((tq, 2 * C_V_DIM), lambda b, h, i: (b * nq + i, h)),
        out_shape=jax.ShapeDtypeStruct((batch * seq, D_MODEL), BF16),
        scratch_shapes=[pltpu.VMEM((2, sp, tq), F32) for sp in spans],
        compiler_params=_params(3),
        name="dilated_attention",
    )(jnp.asarray(_alibi_slopes(len(C_PATTERNS) * C_HEADS)), *([proj] * 6), vt)


def _gelu_tanh(x):
    return 0.5 * x * (1.0 + jnp.tanh(math.sqrt(2.0 / math.pi) * (x + 0.044715 * x * x * x)))


def _compress_body(ak_ref, av_ref, pos_ref, w1_ref, w2_ref, o_ref):
    n = ak_ref.shape[2]
    acc = jnp.zeros((n, LANES), F32)
    for kv, a_ref in enumerate((ak_ref, av_ref)):
        a = a_ref[0, 0].astype(F32)
        first = jnp.dot((a + pos_ref[kv, 0]).astype(BF16), w1_ref[kv, 0], preferred_element_type=F32)
        second = jnp.dot((a + pos_ref[kv, 1]).astype(BF16), w1_ref[kv, 1], preferred_element_type=F32)
        hidden = _gelu_tanh(first + pltpu.roll(second, n - 1, 0))
        acc = acc + jnp.dot(hidden.astype(BF16), w2_ref[kv], preferred_element_type=F32)
    o_ref[0, 0] = acc.astype(BF16)


def _compress(ak, av, pos, w1, w2):
    batch, hk, n, width = ak.shape
    a_spec = pl.BlockSpec((1, 1, n, width), lambda b, h: (b, h, 0, 0))
    return pl.pallas_call(
        _compress_body,
        grid=(batch, hk),
        in_specs=[a_spec, a_spec,
                  pl.BlockSpec(pos.shape, lambda b, h: (0, 0, 0, 0)),
                  pl.BlockSpec(w1.shape, lambda b, h: (0, 0, 0, 0)),
                  pl.BlockSpec(w2.shape, lambda b, h: (0, 0, 0))],
        out_specs=pl.BlockSpec((1, 1, n, LANES), lambda b, h: (b, h, 0, 0)),
        out_shape=jax.ShapeDtypeStruct((batch, hk, n, LANES), BF16),
        compiler_params=_params(2),
        name="nsa_compress",
    )(ak, av, pos, w1, w2)


def _nsa_body(slopes_ref, q_ref, sel_ref, vst_ref, win_ref, vwt_ref, cmp_ref, vct_ref, ovt_ref, kf_ref,
              oc_ref, os_ref, ow_ref, m_sc, acc_sc, qt_sc, neg_sc, u_sc, p_sc, a_sc):
    t, g_n, dh = A_T, A_GROUP, HEAD_DIM
    hk = pl.program_id(1)
    i = pl.program_id(2)
    q_start = i * t
    slopes2 = [slopes_ref[hk * g_n + g] * LOG2E for g in range(g_n)]

    lane = _lane_iota((1, LANES))
    t_lane = q_start + _lane_iota((1, t))

    q32 = q_ref[...].astype(F32) * (dh ** -0.5 * LOG2E)
    qpad = []
    for g in range(g_n):
        blk = q32[:, (g // 2) * LANES:(g // 2 + 1) * LANES]
        if g % 2:
            blk = pltpu.roll(blk, dh, 1)
        qpad.append(jnp.where(lane < dh, blk, 0.0).astype(BF16))

    qt = q32.T
    feat_row = _row_iota((A_FEAT_ROWS // 2, t))
    slope_rows = []
    for g in range(g_n):
        sr = jnp.zeros((A_FEAT_ROWS // 2, t), F32) + slopes2[g]
        hi = sr.astype(BF16).astype(F32)
        lo = (sr - hi).astype(BF16).astype(F32)
        slope_rows.append(jnp.where(feat_row == 0, hi, jnp.where(feat_row == 1, lo, 0.0)))
        for par in range(2):
            qt_sc[par, g, :dh] = qt[g * dh:(g + 1) * dh].astype(BF16)
            qt_sc[par, g, dh + A_FEAT_ROWS:] = jnp.zeros((LANES - dh - A_FEAT_ROWS, t), BF16)

    def store_token_major(ref, per_head):
        ref[...] = jnp.concatenate(per_head, axis=0).T.astype(BF16)

    n_cmp = cmp_ref.shape[2]
    ckv = cmp_ref[0, 0]
    vct = vct_ref[0, 0]
    cmp_end = A_CMP_STRIDE * _row_iota((n_cmp, t)) + (A_CMP_BLOCK - 1)
    cvalid = t_lane >= cmp_end
    cpos = (cmp_end - q_start).astype(F32)
    raw = [lax.dot_general(ckv, qpad[g], _NT, preferred_element_type=F32) for g in range(g_n)]
    probs = []
    for g in range(g_n):
        u = jnp.where(cvalid, raw[g] + slopes2[g] * cpos, NEG_INF)
        e = jnp.where(cvalid, jnp.exp2(u - jnp.max(u, axis=0, keepdims=True)), 0.0)
        probs.append(e * (1.0 / jnp.maximum(jnp.sum(e, axis=0, keepdims=True), TINY)))
    store_token_major(oc_ref, [jnp.dot(vct, p.astype(BF16), preferred_element_type=F32) for p in probs])
    p_sum = (probs[0] + probs[1]) + (probs[2] + probs[3])

    n_sel = ovt_ref.shape[0]
    hi, lo = _split_bf16(p_sum)
    ovt = ovt_ref[...]
    imp = jnp.dot(ovt, hi, preferred_element_type=F32) + jnp.dot(ovt, lo, preferred_element_type=F32)
    jrow = _row_iota((n_sel, t))
    cur = jnp.right_shift(t_lane, A_SEL_SHIFT)
    forced = (jrow == 0) | (jrow == cur) | (jrow == cur - 1)
    imp = jnp.where(jrow > cur, -1.0, imp + jnp.where(forced, A_FORCE_BONUS, 0.0))
    jrow_f = jrow.astype(F32)
    selected = jnp.zeros((n_sel, t), F32)
    for _ in range(min(A_SEL_TOPK, n_sel)):
        top = jnp.max(imp, axis=0, keepdims=True)
        first = jnp.min(jnp.where(imp == top, jrow_f, float(n_sel)), axis=0, keepdims=True)
        pick = jrow_f == first
        selected = jnp.where(pick, 1.0, selected)
        imp = jnp.where(pick, -2.0, imp)
    neg_sc[...] = jnp.where(selected > 0.5, 0.0, NEG_INF)

    ones = jnp.ones((ONES_ROWS, t), BF16)
    blocks_per_pair = 2 * t // A_SEL_BLOCK

    def branch(kv_ref, vt_ref, block_masked, keep_fn, last, out_ref):
        m_sc[...] = jnp.full(m_sc.shape, NEG_INF, F32)
        acc_sc[...] = jnp.zeros(acc_sc.shape, F32)

        def set_features(par, mask_rows):
            for g in range(g_n):
                qt_sc[par, g, dh:dh + A_FEAT_ROWS] = jnp.concatenate([slope_rows[g], mask_rows], axis=0).astype(BF16)

        if not block_masked:
            for par in range(2):
                set_features(par, jnp.zeros((A_FEAT_ROWS // 2, t), F32))

        def score(s, par):
            tile = i - s
            kv = kv_ref[pl.ds(pl.multiple_of(tile * t, t), t), :]
            k_aug = jnp.where(lane < dh, kv, kf_ref[tile % 2])
            if block_masked:
                pair_start = pl.multiple_of((tile // 2) * blocks_per_pair, blocks_per_pair)
                set_features(par, neg_sc[pl.ds(pair_start, blocks_per_pair), :])
            for g in range(g_n):
                u_sc[par, g] = jnp.dot(k_aug, qt_sc[par, g], preferred_element_type=F32)

        def softmax(s, par, diagonal):
            keep = keep_fn(s, diagonal)
            offset = jnp.asarray(-s * t, jnp.int32).astype(F32)
            for g in range(g_n):
                cj = slopes2[g] * offset
                u = u_sc[par, g]
                if keep is not None:
                    u = jnp.where(keep, u, NEG_INF)
                m_old = m_sc[g]
                m_new = jnp.maximum(m_old, jnp.max(u, axis=0, keepdims=True) + cj)
                p_sc[par, g] = jnp.exp2(u - (m_new - cj)).astype(BF16)
                a_sc[par, g] = jnp.exp2(m_old - m_new)
                m_sc[g] = m_new

        def accumulate(s, par):
            vaug = jnp.concatenate([vt_ref[0, 0, i - s], ones], axis=0)
            pv = [jnp.dot(vaug, p_sc[par, g], preferred_element_type=F32) for g in range(g_n)]
            for g in range(g_n):
                acc_sc[g] = a_sc[par, g] * acc_sc[g] + pv[g]

        _pipeline3(last, score, softmax, accumulate)
        store_token_major(out_ref, [acc_sc[g, :dh] / jnp.maximum(acc_sc[g, dh:dh + 1], TINY)
                                    for g in range(g_n)])

    def keep_selected(s, diagonal):
        return _row_iota((t, t)) <= _lane_iota((t, t)) if diagonal else None

    def keep_window(s, diagonal):
        rel = _row_iota((t, t)) - _lane_iota((t, t))
        return rel <= 0 if diagonal else rel > s * t - A_WINDOW

    branch(sel_ref, vst_ref, True, keep_selected, i, os_ref)
    branch(win_ref, vwt_ref, False, keep_window, jnp.minimum(i, A_WINDOW // t), ow_ref)


def _nsa_attention(proj, vst, vwt, cmp_kv, batch, seq):
    t = A_T
    nq = seq // t
    hk_n, dh, g_n = A_KV_HEADS, HEAD_DIM, A_GROUP
    n_cmp = cmp_kv.shape[2]
    n_sel = seq // A_SEL_BLOCK
    ov = np.zeros((n_cmp, n_sel), np.float32)
    real = (seq - A_CMP_BLOCK) // A_CMP_STRIDE + 1
    cidx = A_CMP_STRIDE * np.arange(real)[:, None] + np.arange(A_CMP_BLOCK)[None, :]
    np.add.at(ov, (np.repeat(np.arange(real), A_CMP_BLOCK), (cidx // A_SEL_BLOCK).ravel()), 1.0 / A_CMP_BLOCK)
    qb = D_MODEL // (g_n * dh)
    sb = 2 * D_MODEL // LANES
    wb = sb + hk_n
    vct = cmp_kv[..., dh:].transpose(0, 1, 3, 2)
    kf = np.zeros((2, t, LANES), np.float32)
    kf[:, :, dh] = kf[:, :, dh + 1] = np.arange(t)
    for par in range(2):
        blk = par * (t // A_SEL_BLOCK) + np.arange(t) // A_SEL_BLOCK
        kf[par, np.arange(t), dh + A_FEAT_ROWS // 2 + blk] = 1.0
    vt_spec = pl.BlockSpec((1, 1, nq, dh, t), lambda b, h, i: (b, h, 0, 0, 0))
    o_spec = pl.BlockSpec((t, g_n * dh), lambda b, h, i: (b * nq + i, h))
    o_shape = jax.ShapeDtypeStruct((batch * seq, D_MODEL), BF16)
    return pl.pallas_call(
        _nsa_body,
        grid=(batch, hk_n, nq),
        in_specs=[_smem_spec(),
                  pl.BlockSpec((t, g_n * dh), lambda b, h, i: (b * nq + i, qb + h)),
                  pl.BlockSpec((seq, LANES), lambda b, h, i: (b, sb + h)),
                  vt_spec,
                  pl.BlockSpec((seq, LANES), lambda b, h, i: (b, wb + h)),
                  vt_spec,
                  pl.BlockSpec((1, 1, n_cmp, LANES), lambda b, h, i: (b, h, 0, 0)),
                  pl.BlockSpec((1, 1, dh, n_cmp), lambda b, h, i: (b, h, 0, 0)),
                  pl.BlockSpec((n_sel, n_cmp), lambda b, h, i: (0, 0)),
                  pl.BlockSpec((2, t, LANES), lambda b, h, i: (0, 0, 0))],
        out_specs=[o_spec, o_spec, o_spec],
        out_shape=[o_shape, o_shape, o_shape],
        scratch_shapes=[pltpu.VMEM((g_n, 1, t), F32),
                        pltpu.VMEM((g_n, dh + ONES_ROWS, t), F32),
                        pltpu.VMEM((2, g_n, LANES, t), BF16),
                        pltpu.VMEM((n_sel, t), F32),
                        pltpu.VMEM((2, g_n, t, t), F32),
                        pltpu.VMEM((2, g_n, t, t), BF16),
                        pltpu.VMEM((2, g_n, 1, t), F32)],
        compiler_params=_params(3),
        name="nsa_attention",
    )(jnp.asarray(_alibi_slopes(A_HEADS)), proj, proj, vst, proj, vwt, cmp_kv, vct, jnp.asarray(ov.T, BF16), jnp.asarray(kf, BF16))


def _nsa_layer(h, batch, seq, gain_pre, gain_post, w_in, w_out, pos_k, pos_v, w1_k, w2_k, w1_v, w2_v):
    dm, dh, hk_n = D_MODEL, HEAD_DIM, A_KV_HEADS
    kvw = hk_n * dh
    q0, kc0, vc0, ks0, vs0, kw0, vw0, gl0, z0 = np.cumsum((0, dm) + (kvw,) * 6 + (3 * A_HEADS,))
    cols = [np.arange(z0, z0 + dm), np.arange(q0, q0 + dm)]
    for k0, v0 in ((ks0, vs0), (kw0, vw0)):
        for hh in range(hk_n):
            cols += [np.arange(k0 + hh * dh, k0 + (hh + 1) * dh), np.arange(v0 + hh * dh, v0 + (hh + 1) * dh)]
    cols += [np.arange(kc0, kc0 + kvw), np.arange(vc0, vc0 + kvw), np.arange(gl0, gl0 + 3 * A_HEADS)]
    cols = np.concatenate(cols)
    w = jnp.pad(w_in[:, cols], ((0, 0), (0, LANES - 3 * A_HEADS))).astype(BF16)
    packed = lambda c0: [(c0 + hh * LANES, LANES, dh, dh) for hh in range(hk_n)]
    proj, vst, vwt = _proj_in(h, gain_pre, w, batch, [packed(2 * dm), packed(2 * dm + 2 * kvw)])

    grp = A_CMP_STRIDE
    cmp_col = 2 * dm + 4 * kvw

    def groups(c0):
        a = proj[:, c0:c0 + kvw].reshape(batch, seq // grp, grp, hk_n, dh)
        return a.transpose(0, 3, 1, 2, 4).reshape(batch, hk_n, seq // grp, grp * dh)

    half = grp * dh
    pos = jnp.stack([pos_k.reshape(2, 1, half), pos_v.reshape(2, 1, half)])
    w1 = jnp.stack([w1_k.reshape(2, half, A_CMP_HIDDEN), w1_v.reshape(2, half, A_CMP_HIDDEN)]).astype(BF16)
    w2 = jnp.stack([jnp.pad(w2_k, ((0, 0), (0, dh))), jnp.pad(w2_v, ((0, 0), (dh, 0)))]).astype(BF16)
    cmp_kv = _compress(groups(cmp_col), groups(cmp_col + kvw), pos, w1, w2)

    oc, osel, ow = _nsa_attention(proj, vst, vwt, cmp_kv, batch, seq)
    gl_blk = (cmp_col + 2 * kvw) // LANES
    return _proj_out([oc, osel, ow], proj, 0, w_out.astype(BF16), gain_post, h, gl_blk=gl_blk)


def kernel(x, norm_pre, norm_post, a_w_in, a_w_out, a_cmp_pos_k, a_cmp_pos_v, a_cmp_w1_k, a_cmp_w2_k,
           a_cmp_w1_v, a_cmp_w2_v, b_w_in, b_w_out, b_lambda, b_sub_gain, c_w_in, c_w_out, d_w_in, d_w_out):
    batch, seq, dm = x.shape
    h = x.reshape(batch * seq, dm)
    for i in range(DEPTH):
        mixer, j = i % 4, i // 4
        if mixer == 0:
            h = _nsa_layer(h, batch, seq, norm_pre[i], norm_post[i], a_w_in[j], a_w_out[j],
                           a_cmp_pos_k[j], a_cmp_pos_v[j], a_cmp_w1_k[j], a_cmp_w2_k[j],
                           a_cmp_w1_v[j], a_cmp_w2_v[j])
            continue
        if mixer == 1:
            heads = [(2 * dm + hh * LANES, LANES, 0, LANES) for hh in range(B_HEADS)]
            proj, vt = _proj_in(h, norm_pre[i], b_w_in[j].astype(BF16), batch, [heads])
            lambda_init = 0.8 - 0.6 * math.exp(-0.3 * i)
            o = _diff_attention(proj, vt, b_lambda[j], b_sub_gain[j], lambda_init, batch, seq)
            z_blk, w_out = 3, b_w_out[j]
        elif mixer == 2:
            qk = 2 * len(C_PATTERNS) * C_HEADS * HEAD_DIM
            w = jnp.concatenate([c_w_in[j][:, qk + dm:], c_w_in[j][:, :qk + dm]], axis=1)
            heads = [(dm + qk + hh * C_V_DIM, C_V_DIM, 0, C_V_DIM) for hh in range(C_HEADS)]
            proj, vt = _proj_in(h, norm_pre[i], w.astype(BF16), batch, [heads])
            o = _dilated_attention(proj, vt, batch, seq)
            z_blk, w_out = 0, c_w_out[j]
        else:
            heads = [(2 * dm + hh * LANES, LANES, 0, LANES) for hh in range(D_HEADS // 2)]
            proj, vt = _proj_in(h, norm_pre[i], d_w_in[j].astype(BF16), batch, [heads])
            o = _stick_attention(proj, vt, batch, seq)
            z_blk, w_out = 3, d_w_out[j]
        h = _proj_out([o], proj, z_blk, w_out.astype(BF16), norm_post[i], h)
    return h.reshape(batch, seq, dm)
```

```python
import functools
import math

import jax
import jax.numpy as jnp
import numpy as np
from jax import lax
from jax.experimental import pallas as pl
from jax.experimental.pallas import tpu as pltpu

F32 = jnp.float32
BF16 = jnp.bfloat16

D_MODEL = 1024
HEAD_DIM = 64
DEPTH = 4
RMS_EPS = 1e-6
NEG_INF = -1e30
TINY = 1e-30
LANES = 128
ONES_ROWS = 16
LOG2E = math.log2(math.e)
VMEM_LIMIT = 48 * 1024 * 1024

A_HEADS = 16
A_KV_HEADS = 4
A_GROUP = 4
A_CMP_BLOCK = 32
A_CMP_STRIDE = 16
A_CMP_HIDDEN = 256
A_SEL_BLOCK = 64
A_SEL_SHIFT = 6
A_SEL_TOPK = 16
A_WINDOW = 512
A_FORCE_BONUS = 1e3
A_T = 256
A_FEAT_ROWS = 16
B_HEADS = 8
B_T = 256
C_PATTERNS = ((128, 1), (512, 4), (2048, 16))
C_HEADS = 4
C_V_DIM = 256
C_T = 256
D_HEADS = 16
D_T = 256

PROJ_TM = 512
VT_TILE = 256
PROJ_CHUNK = 512

_NT = (((1,), (1,)), ((), ()))


def _alibi_slopes(n):
    return np.array([2.0 ** (-8.0 * (i + 1) / n) for i in range(n)], np.float32)


def _params(n_grid):
    return pltpu.CompilerParams(dimension_semantics=("arbitrary",) * n_grid,
                                vmem_limit_bytes=VMEM_LIMIT)


def _split_bf16(x):
    hi = x.astype(BF16)
    lo = (x - hi.astype(F32)).astype(BF16)
    return hi, lo


def _sigmoid(x):
    return 1.0 / (1.0 + jnp.exp(-x))


def _proj_in_body(h_ref, g_ref, w_ref, o_ref, *vt_refs, plans):
    x = h_ref[...]
    ms = jnp.mean(x * x, axis=-1, keepdims=True)
    u = (x * lax.rsqrt(ms + RMS_EPS) * g_ref[...]).astype(BF16)
    n = o_ref.shape[1]
    for c in range(0, n, PROJ_CHUNK):
        e = min(c + PROJ_CHUNK, n)
        y = jnp.dot(u, w_ref[:, c:e], preferred_element_type=F32)
        o_ref[:, c:e] = y.astype(BF16)
        for vt_ref, heads in zip(vt_refs, plans):
            for hd, (col, width, row, rows) in enumerate(heads):
                if c <= col and col + width <= e:
                    for r in range(PROJ_TM // VT_TILE):
                        tile = y[r * VT_TILE:(r + 1) * VT_TILE, col - c:col - c + width].T
                        vt_ref[0, hd, r] = tile[row:row + rows].astype(BF16)


def _proj_in(h, gain, w, batch, plans=()):
    m, n = h.shape[0], w.shape[1]
    steps_per_batch = m // batch // PROJ_TM
    vt_shapes = [jax.ShapeDtypeStruct((batch, len(p), m // batch // VT_TILE, p[0][3], VT_TILE), BF16) for p in plans]
    vt_specs = [pl.BlockSpec((1, len(p), PROJ_TM // VT_TILE, p[0][3], VT_TILE),
                             lambda i: (i // steps_per_batch, 0, i % steps_per_batch, 0, 0)) for p in plans]
    outs = pl.pallas_call(
        functools.partial(_proj_in_body, plans=tuple(plans)),
        grid=(m // PROJ_TM,),
        in_specs=[pl.BlockSpec((PROJ_TM, D_MODEL), lambda i: (i, 0)),
                  pl.BlockSpec((1, D_MODEL), lambda i: (0, 0)),
                  pl.BlockSpec((D_MODEL, n), lambda i: (0, 0))],
        out_specs=[pl.BlockSpec((PROJ_TM, n), lambda i: (i, 0))] + vt_specs,
        out_shape=[jax.ShapeDtypeStruct((m, n), BF16)] + vt_shapes,
        compiler_params=_params(1),
        name="proj_in",
    )(h, gain.reshape(1, D_MODEL), w)
    return outs


def _finish(o, z_ref, w_ref, g_ref, h_ref, out_ref):
    z = z_ref[...].astype(F32)
    gated = (o * (z * _sigmoid(z))).astype(BF16)
    y = jnp.dot(gated, w_ref[...], preferred_element_type=F32)
    ms = jnp.mean(y * y, axis=-1, keepdims=True)
    out_ref[...] = h_ref[...] + y * lax.rsqrt(ms + RMS_EPS) * g_ref[...]


def _proj_out_body(o_ref, z_ref, w_ref, g_ref, h_ref, out_ref):
    _finish(o_ref[...].astype(F32), z_ref, w_ref, g_ref, h_ref, out_ref)


def _proj_out_nsa_body(oc_ref, os_ref, ow_ref, gl_ref, eg_ref, z_ref, w_ref, g_ref, h_ref, out_ref):
    hi, lo = _split_bf16(_sigmoid(gl_ref[...].astype(F32)))
    eg = eg_ref[...]
    gates = jnp.dot(hi, eg, preferred_element_type=F32) + jnp.dot(lo, eg, preferred_element_type=F32)
    o = (gates[:, :D_MODEL] * oc_ref[...].astype(F32)
         + gates[:, D_MODEL:2 * D_MODEL] * os_ref[...].astype(F32)
         + gates[:, 2 * D_MODEL:] * ow_ref[...].astype(F32))
    _finish(o, z_ref, w_ref, g_ref, h_ref, out_ref)


def _proj_out(o_list, proj, z_blk, w_out, gain, h, gl_blk=None):
    m = h.shape[0]
    tm = PROJ_TM
    row = lambda i: (i, 0)
    fixed = lambda i: (0, 0)
    o_specs = [pl.BlockSpec((tm, D_MODEL), row) for _ in o_list]
    tail_specs = [pl.BlockSpec((tm, D_MODEL), lambda i: (i, z_blk)),
                  pl.BlockSpec((D_MODEL, D_MODEL), fixed),
                  pl.BlockSpec((1, D_MODEL), fixed),
                  pl.BlockSpec((tm, D_MODEL), row)]
    tail = [proj, w_out, gain.reshape(1, D_MODEL), h]
    if gl_blk is None:
        body, in_specs, args = _proj_out_body, o_specs + tail_specs, list(o_list) + tail
    else:
        eg = np.zeros((LANES, 3 * D_MODEL), np.float32)
        for br in range(3):
            for hd in range(A_HEADS):
                c0 = br * D_MODEL + hd * HEAD_DIM
                eg[br * A_HEADS + hd, c0:c0 + HEAD_DIM] = 1.0
        body = _proj_out_nsa_body
        in_specs = o_specs + [pl.BlockSpec((tm, LANES), lambda i: (i, gl_blk)),
                              pl.BlockSpec((LANES, 3 * D_MODEL), fixed)] + tail_specs
        args = list(o_list) + [proj, jnp.asarray(eg, BF16)] + tail
    return pl.pallas_call(
        body,
        grid=(m // tm,),
        in_specs=in_specs,
        out_specs=pl.BlockSpec((tm, D_MODEL), row),
        out_shape=jax.ShapeDtypeStruct((m, D_MODEL), F32),
        compiler_params=_params(1),
        name="proj_out",
    )(*args)


def _half_masked(q32):
    lane = lax.broadcasted_iota(jnp.int32, (1, LANES), 1)
    lo = jnp.where(lane < HEAD_DIM, q32, 0.0).astype(BF16)
    hi = jnp.where(lane >= HEAD_DIM, q32, 0.0).astype(BF16)
    return lo, hi


def _smem_spec():
    return pl.BlockSpec(memory_space=pltpu.SMEM)


def _row_iota(shape):
    return lax.broadcasted_iota(jnp.int32, shape, 0)


def _lane_iota(shape):
    return lax.broadcasted_iota(jnp.int32, shape, 1)


def _pipeline3(i, score, softmax, accumulate):
    score(0, 0)
    score(jnp.minimum(1, i), 1)
    softmax(0, 0, True)

    def pair(s):
        accumulate(s - 1, 0)
        score(s + 1, 0)
        softmax(s, 1, False)
        accumulate(s, 1)
        score(jnp.minimum(s + 2, i), 1)
        softmax(s + 1, 0, False)

    def quad_body(r, carry):
        pair(4 * r + 1)
        pair(4 * r + 3)
        return carry

    def pair_body(r, carry):
        pair(4 * (i // 4) + 2 * r + 1)
        return carry

    lax.fori_loop(0, i // 4, quad_body, 0)
    lax.fori_loop(0, (i % 4) // 2, pair_body, 0)

    @pl.when(i % 2 == 1)
    def _():
        accumulate(i - 1, 0)
        softmax(i, 1, False)
        accumulate(i, 1)

    @pl.when(i % 2 == 0)
    def _():
        accumulate(i, 0)


def _diff_body(slopes_ref, q_ref, k_ref, vt_ref, lam_ref, sg_ref, o_ref,
               m_sc, acc_sc, b0_sc, u_sc, p_sc, a_sc, *, lambda_init):
    t = B_T
    dv = 2 * HEAD_DIM
    hd = pl.program_id(1)
    i = pl.program_id(2)
    slope2 = slopes_ref[hd] * LOG2E

    @pl.when(i == 0)
    def _():
        b0_sc[...] = slope2 * _row_iota((t, t)).astype(F32)

    qm = _half_masked(q_ref[...].astype(F32) * (HEAD_DIM ** -0.5 * LOG2E))
    m_sc[...] = jnp.full(m_sc.shape, NEG_INF, F32)
    acc_sc[...] = jnp.zeros(acc_sc.shape, F32)
    ones = jnp.ones((ONES_ROWS, t), BF16)

    def score(s, par):
        k = k_ref[pl.ds(pl.multiple_of((i - s) * t, t), t), :]
        for mp in range(2):
            u_sc[par, mp] = lax.dot_general(k, qm[mp], _NT, preferred_element_type=F32)

    def softmax(s, par, diagonal):
        cj = slope2 * jnp.asarray(-s * t, jnp.int32).astype(F32)
        for mp in range(2):
            u = u_sc[par, mp] + b0_sc[...]
            if diagonal:
                u = jnp.where(_row_iota((t, t)) <= _lane_iota((t, t)), u, NEG_INF)
            m_old = m_sc[mp]
            m_new = jnp.maximum(m_old, jnp.max(u, axis=0, keepdims=True) + cj)
            p_sc[par, mp] = jnp.exp2(u - (m_new - cj)).astype(BF16)
            a_sc[par, mp] = jnp.exp2(m_old - m_new)
            m_sc[mp] = m_new

    def accumulate(s, par):
        vaug = jnp.concatenate([vt_ref[0, 0, i - s], ones], axis=0)
        pv = [jnp.dot(vaug, p_sc[par, mp], preferred_element_type=F32) for mp in range(2)]
        for mp in range(2):
            acc_sc[mp] = a_sc[par, mp] * acc_sc[mp] + pv[mp]

    _pipeline3(i, score, softmax, accumulate)

    lam = lam_ref[...]
    lam_full = (jnp.exp(jnp.sum(lam[0:1] * lam[1:2], axis=1, keepdims=True))
                - jnp.exp(jnp.sum(lam[2:3] * lam[3:4], axis=1, keepdims=True)) + lambda_init)
    o0 = acc_sc[0, :dv] / jnp.maximum(acc_sc[0, dv:dv + 1], TINY)
    o1 = acc_sc[1, :dv] / jnp.maximum(acc_sc[1, dv:dv + 1], TINY)
    a = o0 - lam_full * o1
    ms = jnp.mean(a * a, axis=0, keepdims=True)
    y = (a * lax.rsqrt(ms + RMS_EPS) * sg_ref[...]) * (1.0 - lambda_init)
    o_ref[...] = y.T.astype(BF16)


def _diff_attention(proj, vt, lam, sub_gain, lambda_init, batch, seq):
    t = B_T
    nq = seq // t
    nh = B_HEADS
    dv = 2 * HEAD_DIM
    return pl.pallas_call(
        functools.partial(_diff_body, lambda_init=lambda_init),
        grid=(batch, nh, nq),
        in_specs=[_smem_spec(),
                  pl.BlockSpec((t, LANES), lambda b, h, i: (b * nq + i, h)),
                  pl.BlockSpec((seq, LANES), lambda b, h, i: (b, nh + h)),
                  pl.BlockSpec((1, 1, nq, dv, t), lambda b, h, i: (b, h, 0, 0, 0)),
                  pl.BlockSpec((4, HEAD_DIM), lambda b, h, i: (0, 0)),
                  pl.BlockSpec((dv, 1), lambda b, h, i: (0, 0))],
        out_specs=pl.BlockSpec((t, LANES), lambda b, h, i: (b * nq + i, h)),
        out_shape=jax.ShapeDtypeStruct((batch * seq, D_MODEL), BF16),
        scratch_shapes=[pltpu.VMEM((2, 1, t), F32), pltpu.VMEM((2, dv + ONES_ROWS, t), F32),
                        pltpu.VMEM((t, t), F32),
                        pltpu.VMEM((2, 2, t, t), F32), pltpu.VMEM((2, 2, t, t), BF16),
                        pltpu.VMEM((2, 2, 1, t), F32)],
        compiler_params=_params(3),
        name="diff_attention",
    )(jnp.asarray(_alibi_slopes(nh)), proj, proj, vt, lam, sub_gain.reshape(dv, 1))


def _softplus2(x):
    sign = jnp.uint32(0x80000000)
    neg_abs = lax.bitcast_convert_type(lax.bitcast_convert_type(x, jnp.uint32) | sign, F32)
    return jnp.maximum(x, 0.0) + jnp.log(1.0 + jnp.exp2(neg_abs)) * LOG2E


def _stick_body(q_ref, k_ref, vt_ref, tri_ref, o_ref, c_sc, acc_sc, u_sc, ls_sc, hl_sc, a_sc, f_sc):
    t = D_T
    dh = HEAD_DIM
    i = pl.program_id(2)
    qm = _half_masked(q_ref[...].astype(F32) * (dh ** -0.5 * LOG2E))
    c_sc[...] = jnp.zeros(c_sc.shape, F32)
    acc_sc[...] = jnp.zeros(acc_sc.shape, F32)
    a_sc[...] = jnp.zeros(a_sc.shape, BF16)
    f_sc[...] = jnp.zeros(f_sc.shape, F32)

    def logits(s, par):
        k = k_ref[pl.ds(pl.multiple_of((i - s) * t, t), t), :]
        for hh in range(2):
            u_sc[par, hh] = lax.dot_general(k, qm[hh], _NT, preferred_element_type=F32)

    def softplus(par, diagonal):
        for hh in range(2):
            logit = u_sc[par, hh]
            sp = _softplus2(logit)
            log_sig = logit - sp
            if diagonal:
                before = _row_iota((t, t)) < _lane_iota((t, t))
                sp = jnp.where(before, sp, 0.0)
                log_sig = jnp.where(before, log_sig, NEG_INF)
            ls_sc[par, hh] = log_sig
            hl_sc[par, hh] = sp.astype(BF16)

    def suffix_sums(par):
        tri = tri_ref[...]
        return [jnp.dot(tri, hl_sc[par, hh], preferred_element_type=F32) for hh in range(2)]

    def weights(sums, par):
        for hh in range(2):
            a_sc[par, hh] = jnp.exp2(ls_sc[par, hh] + sums[hh][:t]).astype(BF16)
            f_sc[par, hh] = jnp.exp2(c_sc[hh])
            c_sc[hh] = c_sc[hh] + sums[hh][t:t + 1]

    def values(s, par):
        vt = vt_ref[0, 0, jnp.minimum(i - s, i)]
        for hh in range(2):
            acc_sc[hh] = acc_sc[hh] + f_sc[par, hh] * jnp.dot(vt[hh * dh:(hh + 1) * dh], a_sc[par, hh],
                                                              preferred_element_type=F32)

    def iteration(s, par, prefetch):
        sums = suffix_sums(1 - par)
        values(s - 2, par)
        if prefetch:
            logits(jnp.minimum(s + 1, i), 1 - par)
        softplus(par, False)
        weights(sums, 1 - par)

    def drain(par):
        sums = suffix_sums(par)
        values(i - 1, 1 - par)
        weights(sums, par)
        values(i, par)

    logits(0, 0)
    logits(jnp.minimum(1, i), 1)
    softplus(0, True)

    def pair(s):
        iteration(s, 1, True)
        iteration(s + 1, 0, True)

    def quad_body(r, carry):
        pair(4 * r + 1)
        pair(4 * r + 3)
        return carry

    def pair_body(r, carry):
        pair(4 * (i // 4) + 2 * r + 1)
        return carry

    lax.fori_loop(0, i // 4, quad_body, 0)
    lax.fori_loop(0, (i % 4) // 2, pair_body, 0)

    @pl.when(i % 2 == 1)
    def _():
        iteration(i, 1, False)
        drain(1)

    @pl.when(i % 2 == 0)
    def _():
        drain(0)

    o_ref[...] = jnp.concatenate([acc_sc[0], acc_sc[1]], axis=0).T.astype(BF16)


def _stick_attention(proj, vt, batch, seq):
    t = D_T
    nq = seq // t
    nb = D_HEADS // 2
    tri = -np.concatenate([np.triu(np.ones((t, t), np.float32), 1),
                           np.ones((ONES_ROWS, t), np.float32)])
    return pl.pallas_call(
        _stick_body,
        grid=(batch, nb, nq),
        in_specs=[pl.BlockSpec((t, LANES), lambda b, h, i: (b * nq + i, h)),
                  pl.BlockSpec((seq, LANES), lambda b, h, i: (b, nb + h)),
                  pl.BlockSpec((1, 1, nq, LANES, t), lambda b, h, i: (b, h, 0, 0, 0)),
                  pl.BlockSpec((t + ONES_ROWS, t), lambda b, h, i: (0, 0))],
        out_specs=pl.BlockSpec((t, LANES), lambda b, h, i: (b * nq + i, h)),
        out_shape=jax.ShapeDtypeStruct((batch * seq, D_MODEL), BF16),
        scratch_shapes=[pltpu.VMEM((2, 1, t), F32), pltpu.VMEM((2, HEAD_DIM, t), F32),
                        pltpu.VMEM((2, 2, t, t), F32), pltpu.VMEM((2, 2, t, t), F32),
                        pltpu.VMEM((2, 2, t, t), BF16), pltpu.VMEM((2, 2, t, t), BF16),
                        pltpu.VMEM((2, 2, 1, t), F32)],
        compiler_params=_params(3),
        name="stick_attention",
    )(proj, proj, vt, jnp.asarray(tri, BF16))


def _dilated_body(slopes_ref, q0_ref, q1_ref, q2_ref, k0_ref, k1_ref, k2_ref, vt_ref, o_ref,
                  bias0_sc, bias1_sc, bias2_sc):
    tq, dv = C_T, C_V_DIM
    hp = pl.program_id(1)
    i = pl.program_id(2)
    q_start = i * tq
    q_refs = (q0_ref, q1_ref, q2_ref)
    k_refs = (k0_ref, k1_ref, k2_ref)
    bias_scs = (bias0_sc, bias1_sc, bias2_sc)
    seq = k0_ref.shape[0]
    spans = [min(-(-(w + tq) // tq) * tq, seq) for w, _ in C_PATTERNS]
    slopes2 = [[slopes_ref[g * C_HEADS + hp * 2 + hh] * LOG2E for g in range(len(C_PATTERNS))]
               for hh in range(2)]

    for g, (w, d) in enumerate(C_PATTERNS):
        @pl.when(i < spans[g] // tq)
        def _(g=g, w=w, d=d):
            span = spans[g]
            dist = q_start + _lane_iota((span, tq)) - _row_iota((span, tq))
            valid = (dist >= 0) & (dist <= w) & ((dist & (d - 1)) == 0)
            dist_f = dist.astype(F32)
            for hh in range(2):
                bias_scs[g][hh] = jnp.where(valid, -slopes2[hh][g] * dist_f, NEG_INF)

    ones = jnp.ones((ONES_ROWS, tq), BF16)
    for hh in range(2):
        qms = [_half_masked(q_refs[g][...].astype(F32) * (HEAD_DIM ** -0.5 * LOG2E))[hh]
               for g in range(len(C_PATTERNS))]
        starts = [jnp.maximum(q_start + tq - spans[g], 0) for g in range(len(C_PATTERNS))]
        raws = [lax.dot_general(k_refs[g][pl.ds(pl.multiple_of(starts[g], tq), spans[g]), :], qms[g], _NT,
                                preferred_element_type=F32) for g in range(len(C_PATTERNS))]
        outs, lses = [], []
        for g in range(len(C_PATTERNS)):
            span = spans[g]
            u = raws[g] + bias_scs[g][hh]
            m = jnp.max(u, axis=0, keepdims=True)
            e = jnp.exp2(u - m).astype(BF16)
            first_tile = starts[g] // tq
            acc = jnp.zeros((dv + ONES_ROWS, tq), F32)
            for r in range(span // tq):
                vaug = jnp.concatenate([vt_ref[0, hh, first_tile + r], ones], axis=0)
                acc = acc + jnp.dot(vaug, e[r * tq:(r + 1) * tq], preferred_element_type=F32)
            l = acc[dv:dv + 1]
            outs.append(acc[:dv] / l)
            lses.append(m + jnp.log(l) * LOG2E)
        mx = jnp.maximum(jnp.maximum(lses[0], lses[1]), lses[2])
        ws = [jnp.exp2(x - mx) for x in lses]
        mixed = (ws[0] * outs[0] + ws[1] * outs[1] + ws[2] * outs[2]) / (ws[0] + ws[1] + ws[2])
        o_ref[:, hh * dv:(hh + 1) * dv] = mixed.T.astype(BF16)


def _dilated_attention(proj, vt, batch, seq):
    tq = C_T
    nq = seq // tq
    zb = D_MODEL // LANES
    nqk = len(C_PATTERNS) * C_HEADS // 2
    spans = [min(-(-(w + tq) // tq) * tq, seq) for w, _ in C_PATTERNS]
    q_spec = lambda g: pl.BlockSpec((tq, LANES), lambda b, h, i: (b * nq + i, zb + 2 * g + h))
    k_spec = lambda g: pl.BlockSpec((seq, LANES), lambda b, h, i: (b, zb + nqk + 2 * g + h))
    return pl.pallas_call(
        _dilated_body,
        grid=(batch, 2, nq),
        in_specs=[_smem_spec(), q_spec(0), q_spec(1), q_spec(2), k_spec(0), k_spec(1), k_spec(2),
                  pl.BlockSpec((1, 2, nq, C_V_DIM, tq), lambda b, h, i: (b, h, 0, 0, 0))],
        out_specs=pl.BlockSpec((tq, 2 * C_V_DIM), lambda b, h, i: (b * nq + i, h)),
        out_shape=jax.ShapeDtypeStruct((batch * seq, D_MODEL), BF16),
        scratch_shapes=[pltpu.VMEM((2, sp, tq), F32) for sp in spans],
        compiler_params=_params(3),
        name="dilated_attention",
    )(jnp.asarray(_alibi_slopes(len(C_PATTERNS) * C_HEADS)), *([proj] * 6), vt)


def _gelu_tanh(x):
    return 0.5 * x * (1.0 + jnp.tanh(math.sqrt(2.0 / math.pi) * (x + 0.044715 * x * x * x)))


def _compress_body(ak_ref, av_ref, pos_ref, w1_ref, w2_ref, o_ref):
    n = ak_ref.shape[2]
    acc = jnp.zeros((n, LANES), F32)
    for kv, a_ref in enumerate((ak_ref, av_ref)):
        a = a_ref[0, 0].astype(F32)
        first = jnp.dot((a + pos_ref[kv, 0]).astype(BF16), w1_ref[kv, 0], preferred_element_type=F32)
        second = jnp.dot((a + pos_ref[kv, 1]).astype(BF16), w1_ref[kv, 1], preferred_element_type=F32)
        hidden = _gelu_tanh(first + pltpu.roll(second, n - 1, 0))
        acc = acc + jnp.dot(hidden.astype(BF16), w2_ref[kv], preferred_element_type=F32)
    o_ref[0, 0] = acc.astype(BF16)


def _compress(ak, av, pos, w1, w2):
    batch, hk, n, width = ak.shape
    a_spec = pl.BlockSpec((1, 1, n, width), lambda b, h: (b, h, 0, 0))
    return pl.pallas_call(
        _compress_body,
        grid=(batch, hk),
        in_specs=[a_spec, a_spec,
                  pl.BlockSpec(pos.shape, lambda b, h: (0, 0, 0, 0)),
                  pl.BlockSpec(w1.shape, lambda b, h: (0, 0, 0, 0)),
                  pl.BlockSpec(w2.shape, lambda b, h: (0, 0, 0))],
        out_specs=pl.BlockSpec((1, 1, n, LANES), lambda b, h: (b, h, 0, 0)),
        out_shape=jax.ShapeDtypeStruct((batch, hk, n, LANES), BF16),
        compiler_params=_params(2),
        name="nsa_compress",
    )(ak, av, pos, w1, w2)


def _nsa_body(slopes_ref, q_ref, sel_ref, vst_ref, win_ref, vwt_ref, cmp_ref, vct_ref, ovt_ref, kf_ref,
              oc_ref, os_ref, ow_ref, m_sc, acc_sc, qt_sc, neg_sc, u_sc, p_sc, a_sc):
    t, g_n, dh = A_T, A_GROUP, HEAD_DIM
    hk = pl.program_id(1)
    i = pl.program_id(2)
    q_start = i * t
    slopes2 = [slopes_ref[hk * g_n + g] * LOG2E for g in range(g_n)]

    lane = _lane_iota((1, LANES))
    t_lane = q_start + _lane_iota((1, t))

    q32 = q_ref[...].astype(F32) * (dh ** -0.5 * LOG2E)
    qpad = []
    for g in range(g_n):
        blk = q32[:, (g // 2) * LANES:(g // 2 + 1) * LANES]
        if g % 2:
            blk = pltpu.roll(blk, dh, 1)
        qpad.append(jnp.where(lane < dh, blk, 0.0).astype(BF16))

    qt = q32.T
    feat_row = _row_iota((A_FEAT_ROWS // 2, t))
    slope_rows = []
    for g in range(g_n):
        sr = jnp.zeros((A_FEAT_ROWS // 2, t), F32) + slopes2[g]
        hi = sr.astype(BF16).astype(F32)
        lo = (sr - hi).astype(BF16).astype(F32)
        slope_rows.append(jnp.where(feat_row == 0, hi, jnp.where(feat_row == 1, lo, 0.0)))
        for par in range(2):
            qt_sc[par, g, :dh] = qt[g * dh:(g + 1) * dh].astype(BF16)
            qt_sc[par, g, dh + A_FEAT_ROWS:] = jnp.zeros((LANES - dh - A_FEAT_ROWS, t), BF16)

    def store_token_major(ref, per_head):
        ref[...] = jnp.concatenate(per_head, axis=0).T.astype(BF16)

    n_cmp = cmp_ref.shape[2]
    ckv = cmp_ref[0, 0]
    vct = vct_ref[0, 0]
    cmp_end = A_CMP_STRIDE * _row_iota((n_cmp, t)) + (A_CMP_BLOCK - 1)
    cvalid = t_lane >= cmp_end
    cpos = (cmp_end - q_start).astype(F32)
    raw = [lax.dot_general(ckv, qpad[g], _NT, preferred_element_type=F32) for g in range(g_n)]
    probs = []
    for g in range(g_n):
        u = jnp.where(cvalid, raw[g] + slopes2[g] * cpos, NEG_INF)
        e = jnp.where(cvalid, jnp.exp2(u - jnp.max(u, axis=0, keepdims=True)), 0.0)
        probs.append(e * (1.0 / jnp.maximum(jnp.sum(e, axis=0, keepdims=True), TINY)))
    store_token_major(oc_ref, [jnp.dot(vct, p.astype(BF16), preferred_element_type=F32) for p in probs])
    p_sum = (probs[0] + probs[1]) + (probs[2] + probs[3])

    n_win = A_WINDOW // t + 1
    ones = jnp.ones((ONES_ROWS, t), BF16)
    rel = _row_iota((t, t)) - _lane_iota((t, t))
    feat_zero = jnp.zeros((A_FEAT_ROWS // 2, t), F32)
    qts = [jnp.concatenate([qt[g * dh:(g + 1) * dh], slope_rows[g], feat_zero,
                            jnp.zeros((LANES - dh - A_FEAT_ROWS, t), F32)], axis=0).astype(BF16) for g in range(g_n)]
    w_tiles, w_keep, w_shift = [], [], []
    for s in range(n_win):
        tile = jnp.maximum(i - s, 0)
        kv = win_ref[pl.ds(pl.multiple_of(tile * t, t), t), :]
        w_tiles.append((tile, jnp.where(lane < dh, kv, kf_ref[0])))
        low = jnp.where(i >= s, s * t - A_WINDOW, 2 * t)
        w_keep.append((rel > low) & (rel <= s * t))
        w_shift.append(float(-s * t))
    w_outs = []
    w_raw = [[jnp.dot(k_aug, qts[g], preferred_element_type=F32) for _, k_aug in w_tiles] for g in range(g_n)]
    for g in range(g_n):
        us = [jnp.where(w_keep[s], w_raw[g][s] + slopes2[g] * w_shift[s], NEG_INF) for s in range(n_win)]
        m = us[0].max(axis=0, keepdims=True)
        for s in range(1, n_win):
            m = jnp.maximum(m, us[s].max(axis=0, keepdims=True))
        acc = jnp.zeros((dh + ONES_ROWS, t), F32)
        for s in range(n_win):
            vaug = jnp.concatenate([vwt_ref[0, 0, w_tiles[s][0]], ones], axis=0)
            acc = acc + jnp.dot(vaug, jnp.exp2(us[s] - m).astype(BF16), preferred_element_type=F32)
        w_outs.append(acc[:dh] / jnp.maximum(acc[dh:dh + 1], TINY))
    store_token_major(ow_ref, w_outs)

    n_sel = ovt_ref.shape[0]
    hi, lo = _split_bf16(p_sum)
    ovt = ovt_ref[...]
    imp = jnp.dot(ovt, hi, preferred_element_type=F32) + jnp.dot(ovt, lo, preferred_element_type=F32)
    jrow = _row_iota((n_sel, t))
    cur = jnp.right_shift(t_lane, A_SEL_SHIFT)
    forced = (jrow == 0) | (jrow == cur) | (jrow == cur - 1)
    imp = jnp.where(jrow > cur, -1.0, imp + jnp.where(forced, A_FORCE_BONUS, 0.0))
    jrow_f = jrow.astype(F32)
    selected = jnp.zeros((n_sel, t), F32)
    for _ in range(min(A_SEL_TOPK, n_sel)):
        top = jnp.max(imp, axis=0, keepdims=True)
        first = jnp.min(jnp.where(imp == top, jrow_f, float(n_sel)), axis=0, keepdims=True)
        pick = jrow_f == first
        selected = jnp.where(pick, 1.0, selected)
        imp = jnp.where(pick, -2.0, imp)
    neg_sc[...] = jnp.where(selected > 0.5, 0.0, NEG_INF)

    blocks_per_pair = 2 * t // A_SEL_BLOCK
    m_sc[...] = jnp.full(m_sc.shape, NEG_INF, F32)
    acc_sc[...] = jnp.zeros(acc_sc.shape, F32)

    def score(s, par):
        tile = i - s
        kv = sel_ref[pl.ds(pl.multiple_of(tile * t, t), t), :]
        k_aug = jnp.where(lane < dh, kv, kf_ref[tile % 2])
        pair_start = pl.multiple_of((tile // 2) * blocks_per_pair, blocks_per_pair)
        mask_rows = neg_sc[pl.ds(pair_start, blocks_per_pair), :]
        for g in range(g_n):
            qt_sc[par, g, dh:dh + A_FEAT_ROWS] = jnp.concatenate([slope_rows[g], mask_rows], axis=0).astype(BF16)
            u_sc[par, g] = jnp.dot(k_aug, qt_sc[par, g], preferred_element_type=F32)

    def softmax(s, par, diagonal):
        offset = jnp.asarray(-s * t, jnp.int32).astype(F32)
        for g in range(g_n):
            cj = slopes2[g] * offset
            u = u_sc[par, g]
            if diagonal:
                u = jnp.where(rel <= 0, u, NEG_INF)
            m_old = m_sc[g]
            m_new = jnp.maximum(m_old, jnp.max(u, axis=0, keepdims=True) + cj)
            p_sc[par, g] = jnp.exp2(u - (m_new - cj)).astype(BF16)
            a_sc[par, g] = jnp.exp2(m_old - m_new)
            m_sc[g] = m_new

    def accumulate(s, par):
        vaug = jnp.concatenate([vst_ref[0, 0, i - s], ones], axis=0)
        pv = [jnp.dot(vaug, p_sc[par, g], preferred_element_type=F32) for g in range(g_n)]
        for g in range(g_n):
            acc_sc[g] = a_sc[par, g] * acc_sc[g] + pv[g]

    _pipeline3(i, score, softmax, accumulate)
    store_token_major(os_ref, [acc_sc[g, :dh] / jnp.maximum(acc_sc[g, dh:dh + 1], TINY) for g in range(g_n)])


def _nsa_attention(proj, vst, vwt, cmp_kv, batch, seq):
    t = A_T
    nq = seq // t
    hk_n, dh, g_n = A_KV_HEADS, HEAD_DIM, A_GROUP
    n_cmp = cmp_kv.shape[2]
    n_sel = seq // A_SEL_BLOCK
    ov = np.zeros((n_cmp, n_sel), np.float32)
    real = (seq - A_CMP_BLOCK) // A_CMP_STRIDE + 1
    cidx = A_CMP_STRIDE * np.arange(real)[:, None] + np.arange(A_CMP_BLOCK)[None, :]
    np.add.at(ov, (np.repeat(np.arange(real), A_CMP_BLOCK), (cidx // A_SEL_BLOCK).ravel()), 1.0 / A_CMP_BLOCK)
    qb = D_MODEL // (g_n * dh)
    sb = 2 * D_MODEL // LANES
    wb = sb + hk_n
    vct = cmp_kv[..., dh:].transpose(0, 1, 3, 2)
    kf = np.zeros((2, t, LANES), np.float32)
    kf[:, :, dh] = kf[:, :, dh + 1] = np.arange(t)
    for par in range(2):
        blk = par * (t // A_SEL_BLOCK) + np.arange(t) // A_SEL_BLOCK
        kf[par, np.arange(t), dh + A_FEAT_ROWS // 2 + blk] = 1.0
    vt_spec = pl.BlockSpec((1, 1, nq, dh, t), lambda b, h, i: (b, h, 0, 0, 0))
    o_spec = pl.BlockSpec((t, g_n * dh), lambda b, h, i: (b * nq + i, h))
    o_shape = jax.ShapeDtypeStruct((batch * seq, D_MODEL), BF16)
    return pl.pallas_call(
        _nsa_body,
        grid=(batch, hk_n, nq),
        in_specs=[_smem_spec(),
                  pl.BlockSpec((t, g_n * dh), lambda b, h, i: (b * nq + i, qb + h)),
                  pl.BlockSpec((seq, LANES), lambda b, h, i: (b, sb + h)),
                  vt_spec,
                  pl.BlockSpec((seq, LANES), lambda b, h, i: (b, wb + h)),
                  vt_spec,
                  pl.BlockSpec((1, 1, n_cmp, LANES), lambda b, h, i: (b, h, 0, 0)),
                  pl.BlockSpec((1, 1, dh, n_cmp), lambda b, h, i: (b, h, 0, 0)),
                  pl.BlockSpec((n_sel, n_cmp), lambda b, h, i: (0, 0)),
                  pl.BlockSpec((2, t, LANES), lambda b, h, i: (0, 0, 0))],
        out_specs=[o_spec, o_spec, o_spec],
        out_shape=[o_shape, o_shape, o_shape],
        scratch_shapes=[pltpu.VMEM((g_n, 1, t), F32),
                        pltpu.VMEM((g_n, dh + ONES_ROWS, t), F32),
                        pltpu.VMEM((2, g_n, LANES, t), BF16),
                        pltpu.VMEM((n_sel, t), F32),
                        pltpu.VMEM((2, g_n, t, t), F32),
                        pltpu.VMEM((2, g_n, t, t), BF16),
                        pltpu.VMEM((2, g_n, 1, t), F32)],
        compiler_params=_params(3),
        name="nsa_attention",
    )(jnp.asarray(_alibi_slopes(A_HEADS)), proj, proj, vst, proj, vwt, cmp_kv, vct, jnp.asarray(ov.T, BF16), jnp.asarray(kf, BF16))


def _nsa_layer(h, batch, seq, gain_pre, gain_post, w_in, w_out, pos_k, pos_v, w1_k, w2_k, w1_v, w2_v):
    dm, dh, hk_n = D_MODEL, HEAD_DIM, A_KV_HEADS
    kvw = hk_n * dh
    q0, kc0, vc0, ks0, vs0, kw0, vw0, gl0, z0 = np.cumsum((0, dm) + (kvw,) * 6 + (3 * A_HEADS,))
    cols = [np.arange(z0, z0 + dm), np.arange(q0, q0 + dm)]
    for k0, v0 in ((ks0, vs0), (kw0, vw0)):
        for hh in range(hk_n):
            cols += [np.arange(k0 + hh * dh, k0 + (hh + 1) * dh), np.arange(v0 + hh * dh, v0 + (hh + 1) * dh)]
    cols += [np.arange(kc0, kc0 + kvw), np.arange(vc0, vc0 + kvw), np.arange(gl0, gl0 + 3 * A_HEADS)]
    cols = np.concatenate(cols)
    w = jnp.pad(w_in[:, cols], ((0, 0), (0, LANES - 3 * A_HEADS))).astype(BF16)
    packed = lambda c0: [(c0 + hh * LANES, LANES, dh, dh) for hh in range(hk_n)]
    proj, vst, vwt = _proj_in(h, gain_pre, w, batch, [packed(2 * dm), packed(2 * dm + 2 * kvw)])

    grp = A_CMP_STRIDE
    cmp_col = 2 * dm + 4 * kvw

    def groups(c0):
        a = proj[:, c0:c0 + kvw].reshape(batch, seq // grp, grp, hk_n, dh)
        return a.transpose(0, 3, 1, 2, 4).reshape(batch, hk_n, seq // grp, grp * dh)

    half = grp * dh
    pos = jnp.stack([pos_k.reshape(2, 1, half), pos_v.reshape(2, 1, half)])
    w1 = jnp.stack([w1_k.reshape(2, half, A_CMP_HIDDEN), w1_v.reshape(2, half, A_CMP_HIDDEN)]).astype(BF16)
    w2 = jnp.stack([jnp.pad(w2_k, ((0, 0), (0, dh))), jnp.pad(w2_v, ((0, 0), (dh, 0)))]).astype(BF16)
    cmp_kv = _compress(groups(cmp_col), groups(cmp_col + kvw), pos, w1, w2)

    oc, osel, ow = _nsa_attention(proj, vst, vwt, cmp_kv, batch, seq)
    gl_blk = (cmp_col + 2 * kvw) // LANES
    return _proj_out([oc, osel, ow], proj, 0, w_out.astype(BF16), gain_post, h, gl_blk=gl_blk)


def kernel(x, norm_pre, norm_post, a_w_in, a_w_out, a_cmp_pos_k, a_cmp_pos_v, a_cmp_w1_k, a_cmp_w2_k,
           a_cmp_w1_v, a_cmp_w2_v, b_w_in, b_w_out, b_lambda, b_sub_gain, c_w_in, c_w_out, d_w_in, d_w_out):
    batch, seq, dm = x.shape
    h = x.reshape(batch * seq, dm)
    for i in range(DEPTH):
        mixer, j = i % 4, i // 4
        if mixer == 0:
            h = _nsa_layer(h, batch, seq, norm_pre[i], norm_post[i], a_w_in[j], a_w_out[j],
                           a_cmp_pos_k[j], a_cmp_pos_v[j], a_cmp_w1_k[j], a_cmp_w2_k[j],
                           a_cmp_w1_v[j], a_cmp_w2_v[j])
            continue
        if mixer == 1:
            heads = [(2 * dm + hh * LANES, LANES, 0, LANES) for hh in range(B_HEADS)]
            proj, vt = _proj_in(h, norm_pre[i], b_w_in[j].astype(BF16), batch, [heads])
            lambda_init = 0.8 - 0.6 * math.exp(-0.3 * i)
            o = _diff_attention(proj, vt, b_lambda[j], b_sub_gain[j], lambda_init, batch, seq)
            z_blk, w_out = 3, b_w_out[j]
        elif mixer == 2:
            qk = 2 * len(C_PATTERNS) * C_HEADS * HEAD_DIM
            w = jnp.concatenate([c_w_in[j][:, qk + dm:], c_w_in[j][:, :qk + dm]], axis=1)
            heads = [(dm + qk + hh * C_V_DIM, C_V_DIM, 0, C_V_DIM) for hh in range(C_HEADS)]
            proj, vt = _proj_in(h, norm_pre[i], w.astype(BF16), batch, [heads])
            o = _dilated_attention(proj, vt, batch, seq)
            z_blk, w_out = 0, c_w_out[j]
        else:
            heads = [(2 * dm + hh * LANES, LANES, 0, LANES) for hh in range(D_HEADS // 2)]
            proj, vt = _proj_in(h, norm_pre[i], d_w_in[j].astype(BF16), batch, [heads])
            o = _stick_attention(proj, vt, batch, seq)
            z_blk, w_out = 3, d_w_out[j]
        h = _proj_out([o], proj, z_blk, w_out.astype(BF16), norm_post[i], h)
    return h.reshape(batch, seq, dm)
```

```python
import functools
import math

import jax
import jax.numpy as jnp
import numpy as np
from jax import lax
from jax.experimental import pallas as pl
from jax.experimental.pallas import tpu as pltpu

F32 = jnp.float32
BF16 = jnp.bfloat16

D_MODEL = 1024
HEAD_DIM = 64
DEPTH = 4
RMS_EPS = 1e-6
NEG_INF = -1e30
TINY = 1e-30
LANES = 128
ONES_ROWS = 16
LOG2E = math.log2(math.e)
VMEM_LIMIT = 48 * 1024 * 1024

A_HEADS = 16
A_KV_HEADS = 4
A_GROUP = 4
A_CMP_BLOCK = 32
A_CMP_STRIDE = 16
A_CMP_HIDDEN = 256
A_SEL_BLOCK = 64
A_SEL_SHIFT = 6
A_SEL_TOPK = 16
A_WINDOW = 512
A_FORCE_BONUS = 1e3
A_T = 256
A_FEAT_ROWS = 16
B_HEADS = 8
B_T = 256
B_PER_STEP = 2
C_PATTERNS = ((128, 1), (512, 4), (2048, 16))
C_HEADS = 4
C_V_DIM = 256
C_T = 256
D_HEADS = 16
D_T = 256

PROJ_TM = 512
VT_TILE = 256
PROJ_CHUNK = 512

_NT = (((1,), (1,)), ((), ()))


def _alibi_slopes(n):
    return np.array([2.0 ** (-8.0 * (i + 1) / n) for i in range(n)], np.float32)


def _params(n_grid):
    return pltpu.CompilerParams(dimension_semantics=("arbitrary",) * n_grid,
                                vmem_limit_bytes=VMEM_LIMIT)


def _split_bf16(x):
    hi = x.astype(BF16)
    lo = (x - hi.astype(F32)).astype(BF16)
    return hi, lo


def _sigmoid(x):
    return 1.0 / (1.0 + jnp.exp(-x))


def _proj_in_body(h_ref, g_ref, w_ref, o_ref, *vt_refs, plans):
    x = h_ref[...]
    ms = jnp.mean(x * x, axis=-1, keepdims=True)
    u = (x * lax.rsqrt(ms + RMS_EPS) * g_ref[...]).astype(BF16)
    n = o_ref.shape[1]
    for c in range(0, n, PROJ_CHUNK):
        e = min(c + PROJ_CHUNK, n)
        y = jnp.dot(u, w_ref[:, c:e], preferred_element_type=F32)
        o_ref[:, c:e] = y.astype(BF16)
        for vt_ref, heads in zip(vt_refs, plans):
            for hd, (col, width, row, rows) in enumerate(heads):
                if c <= col and col + width <= e:
                    for r in range(PROJ_TM // VT_TILE):
                        tile = y[r * VT_TILE:(r + 1) * VT_TILE, col - c:col - c + width].T
                        vt_ref[0, hd, r] = tile[row:row + rows].astype(BF16)


def _proj_in(h, gain, w, batch, plans=()):
    m, n = h.shape[0], w.shape[1]
    steps_per_batch = m // batch // PROJ_TM
    vt_shapes = [jax.ShapeDtypeStruct((batch, len(p), m // batch // VT_TILE, p[0][3], VT_TILE), BF16) for p in plans]
    vt_specs = [pl.BlockSpec((1, len(p), PROJ_TM // VT_TILE, p[0][3], VT_TILE),
                             lambda i: (i // steps_per_batch, 0, i % steps_per_batch, 0, 0)) for p in plans]
    outs = pl.pallas_call(
        functools.partial(_proj_in_body, plans=tuple(plans)),
        grid=(m // PROJ_TM,),
        in_specs=[pl.BlockSpec((PROJ_TM, D_MODEL), lambda i: (i, 0)),
                  pl.BlockSpec((1, D_MODEL), lambda i: (0, 0)),
                  pl.BlockSpec((D_MODEL, n), lambda i: (0, 0))],
        out_specs=[pl.BlockSpec((PROJ_TM, n), lambda i: (i, 0))] + vt_specs,
        out_shape=[jax.ShapeDtypeStruct((m, n), BF16)] + vt_shapes,
        compiler_params=_params(1),
        name="proj_in",
    )(h, gain.reshape(1, D_MODEL), w)
    return outs


def _finish(o, z_ref, w_ref, g_ref, h_ref, out_ref):
    z = z_ref[...].astype(F32)
    gated = (o * (z * _sigmoid(z))).astype(BF16)
    y = jnp.dot(gated, w_ref[...], preferred_element_type=F32)
    ms = jnp.mean(y * y, axis=-1, keepdims=True)
    out_ref[...] = h_ref[...] + y * lax.rsqrt(ms + RMS_EPS) * g_ref[...]


def _proj_out_body(o_ref, z_ref, w_ref, g_ref, h_ref, out_ref):
    _finish(o_ref[...].astype(F32), z_ref, w_ref, g_ref, h_ref, out_ref)


def _proj_out_nsa_body(oc_ref, os_ref, ow_ref, gl_ref, eg_ref, z_ref, w_ref, g_ref, h_ref, out_ref):
    hi, lo = _split_bf16(_sigmoid(gl_ref[...].astype(F32)))
    eg = eg_ref[...]
    gates = jnp.dot(hi, eg, preferred_element_type=F32) + jnp.dot(lo, eg, preferred_element_type=F32)
    o = (gates[:, :D_MODEL] * oc_ref[...].astype(F32)
         + gates[:, D_MODEL:2 * D_MODEL] * os_ref[...].astype(F32)
         + gates[:, 2 * D_MODEL:] * ow_ref[...].astype(F32))
    _finish(o, z_ref, w_ref, g_ref, h_ref, out_ref)


def _proj_out(o_list, proj, z_blk, w_out, gain, h, gl_blk=None):
    m = h.shape[0]
    tm = PROJ_TM
    row = lambda i: (i, 0)
    fixed = lambda i: (0, 0)
    o_specs = [pl.BlockSpec((tm, D_MODEL), row) for _ in o_list]
    tail_specs = [pl.BlockSpec((tm, D_MODEL), lambda i: (i, z_blk)),
                  pl.BlockSpec((D_MODEL, D_MODEL), fixed),
                  pl.BlockSpec((1, D_MODEL), fixed),
                  pl.BlockSpec((tm, D_MODEL), row)]
    tail = [proj, w_out, gain.reshape(1, D_MODEL), h]
    if gl_blk is None:
        body, in_specs, args = _proj_out_body, o_specs + tail_specs, list(o_list) + tail
    else:
        eg = np.zeros((LANES, 3 * D_MODEL), np.float32)
        for br in range(3):
            for hd in range(A_HEADS):
                c0 = br * D_MODEL + hd * HEAD_DIM
                eg[br * A_HEADS + hd, c0:c0 + HEAD_DIM] = 1.0
        body = _proj_out_nsa_body
        in_specs = o_specs + [pl.BlockSpec((tm, LANES), lambda i: (i, gl_blk)),
                              pl.BlockSpec((LANES, 3 * D_MODEL), fixed)] + tail_specs
        args = list(o_list) + [proj, jnp.asarray(eg, BF16)] + tail
    return pl.pallas_call(
        body,
        grid=(m // tm,),
        in_specs=in_specs,
        out_specs=pl.BlockSpec((tm, D_MODEL), row),
        out_shape=jax.ShapeDtypeStruct((m, D_MODEL), F32),
        compiler_params=_params(1),
        name="proj_out",
    )(*args)


def _half_masked(q32):
    lane = lax.broadcasted_iota(jnp.int32, (1, LANES), 1)
    lo = jnp.where(lane < HEAD_DIM, q32, 0.0).astype(BF16)
    hi = jnp.where(lane >= HEAD_DIM, q32, 0.0).astype(BF16)
    return lo, hi


def _smem_spec():
    return pl.BlockSpec(memory_space=pltpu.SMEM)


def _row_iota(shape):
    return lax.broadcasted_iota(jnp.int32, shape, 0)


def _lane_iota(shape):
    return lax.broadcasted_iota(jnp.int32, shape, 1)


def _pipeline3(i, score, softmax, accumulate):
    score(0, 0)
    score(jnp.minimum(1, i), 1)
    softmax(0, 0, True)

    def pair(s):
        accumulate(s - 1, 0)
        score(s + 1, 0)
        softmax(s, 1, False)
        accumulate(s, 1)
        score(jnp.minimum(s + 2, i), 1)
        softmax(s + 1, 0, False)

    def quad_body(r, carry):
        pair(4 * r + 1)
        pair(4 * r + 3)
        return carry

    def pair_body(r, carry):
        pair(4 * (i // 4) + 2 * r + 1)
        return carry

    lax.fori_loop(0, i // 4, quad_body, 0)
    lax.fori_loop(0, (i % 4) // 2, pair_body, 0)

    @pl.when(i % 2 == 1)
    def _():
        accumulate(i - 1, 0)
        softmax(i, 1, False)
        accumulate(i, 1)

    @pl.when(i % 2 == 0)
    def _():
        accumulate(i, 0)


def _diff_body(slopes_ref, q_ref, k_ref, vt_ref, lam_ref, sg_ref, o_ref,
               m_sc, acc_sc, b0_sc, u_sc, p_sc, a_sc, *, lambda_init):
    t = B_T
    dv = 2 * HEAD_DIM
    hd = pl.program_id(1)
    i = pl.program_id(2)
    slopes2 = [slopes_ref[B_PER_STEP * hd + hh] * LOG2E for hh in range(B_PER_STEP)]
    chains = [(hh, mp) for hh in range(B_PER_STEP) for mp in range(2)]

    @pl.when(i == 0)
    def _():
        rows = _row_iota((t, t)).astype(F32)
        for hh in range(B_PER_STEP):
            b0_sc[hh] = slopes2[hh] * rows

    q32 = q_ref[...].astype(F32) * (HEAD_DIM ** -0.5 * LOG2E)
    qm = [_half_masked(q32[:, hh * LANES:(hh + 1) * LANES])[mp] for hh, mp in chains]
    m_sc[...] = jnp.full(m_sc.shape, NEG_INF, F32)
    acc_sc[...] = jnp.zeros(acc_sc.shape, F32)
    ones = jnp.ones((ONES_ROWS, t), BF16)

    def score(s, par):
        k = k_ref[pl.ds(pl.multiple_of((i - s) * t, t), t), :]
        for c, (hh, mp) in enumerate(chains):
            u_sc[par, c] = lax.dot_general(k[:, hh * LANES:(hh + 1) * LANES], qm[c], _NT,
                                           preferred_element_type=F32)

    def softmax(s, par, diagonal):
        offset = jnp.asarray(-s * t, jnp.int32).astype(F32)
        for c, (hh, mp) in enumerate(chains):
            cj = slopes2[hh] * offset
            u = u_sc[par, c] + b0_sc[hh]
            if diagonal:
                u = jnp.where(_row_iota((t, t)) <= _lane_iota((t, t)), u, NEG_INF)
            m_old = m_sc[c]
            m_new = jnp.maximum(m_old, jnp.max(u, axis=0, keepdims=True) + cj)
            p_sc[par, c] = jnp.exp2(u - (m_new - cj)).astype(BF16)
            a_sc[par, c] = jnp.exp2(m_old - m_new)
            m_sc[c] = m_new

    def accumulate(s, par):
        vaug = [jnp.concatenate([vt_ref[0, hh, i - s], ones], axis=0) for hh in range(B_PER_STEP)]
        pv = [jnp.dot(vaug[hh], p_sc[par, c], preferred_element_type=F32) for c, (hh, mp) in enumerate(chains)]
        for c in range(len(chains)):
            acc_sc[c] = a_sc[par, c] * acc_sc[c] + pv[c]

    _pipeline3(i, score, softmax, accumulate)

    lam = lam_ref[...]
    lam_full = (jnp.exp(jnp.sum(lam[0:1] * lam[1:2], axis=1, keepdims=True))
                - jnp.exp(jnp.sum(lam[2:3] * lam[3:4], axis=1, keepdims=True)) + lambda_init)
    for hh in range(B_PER_STEP):
        o0 = acc_sc[2 * hh, :dv] / jnp.maximum(acc_sc[2 * hh, dv:dv + 1], TINY)
        o1 = acc_sc[2 * hh + 1, :dv] / jnp.maximum(acc_sc[2 * hh + 1, dv:dv + 1], TINY)
        a = o0 - lam_full * o1
        ms = jnp.mean(a * a, axis=0, keepdims=True)
        y = (a * lax.rsqrt(ms + RMS_EPS) * sg_ref[...]) * (1.0 - lambda_init)
        o_ref[:, hh * dv:(hh + 1) * dv] = y.T.astype(BF16)


def _diff_attention(proj, vt, lam, sub_gain, lambda_init, batch, seq):
    t = B_T
    nq = seq // t
    per = B_PER_STEP
    ng = B_HEADS // per
    dv = 2 * HEAD_DIM
    n_chain = 2 * per
    return pl.pallas_call(
        functools.partial(_diff_body, lambda_init=lambda_init),
        grid=(batch, ng, nq),
        in_specs=[_smem_spec(),
                  pl.BlockSpec((t, per * dv), lambda b, h, i: (b * nq + i, h)),
                  pl.BlockSpec((seq, per * dv), lambda b, h, i: (b, ng + h)),
                  pl.BlockSpec((1, per, nq, dv, t), lambda b, h, i: (b, h, 0, 0, 0)),
                  pl.BlockSpec((4, HEAD_DIM), lambda b, h, i: (0, 0)),
                  pl.BlockSpec((dv, 1), lambda b, h, i: (0, 0))],
        out_specs=pl.BlockSpec((t, per * dv), lambda b, h, i: (b * nq + i, h)),
        out_shape=jax.ShapeDtypeStruct((batch * seq, D_MODEL), BF16),
        scratch_shapes=[pltpu.VMEM((n_chain, 1, t), F32), pltpu.VMEM((n_chain, dv + ONES_ROWS, t), F32),
                        pltpu.VMEM((per, t, t), F32),
                        pltpu.VMEM((2, n_chain, t, t), F32), pltpu.VMEM((2, n_chain, t, t), BF16),
                        pltpu.VMEM((2, n_chain, 1, t), F32)],
        compiler_params=_params(3),
        name="diff_attention",
    )(jnp.asarray(_alibi_slopes(B_HEADS)), proj, proj, vt, lam, sub_gain.reshape(dv, 1))


def _softplus2(x):
    sign = jnp.uint32(0x80000000)
    neg_abs = lax.bitcast_convert_type(lax.bitcast_convert_type(x, jnp.uint32) | sign, F32)
    return jnp.maximum(x, 0.0) + jnp.log(1.0 + jnp.exp2(neg_abs)) * LOG2E


def _stick_body(q_ref, k_ref, vt_ref, tri_ref, o_ref, c_sc, acc_sc, u_sc, ls_sc, hl_sc, a_sc, f_sc):
    t = D_T
    dh = HEAD_DIM
    i = pl.program_id(2)
    qm = _half_masked(q_ref[...].astype(F32) * (dh ** -0.5 * LOG2E))
    c_sc[...] = jnp.zeros(c_sc.shape, F32)
    acc_sc[...] = jnp.zeros(acc_sc.shape, F32)
    a_sc[...] = jnp.zeros(a_sc.shape, BF16)
    f_sc[...] = jnp.zeros(f_sc.shape, F32)

    def logits(s, par):
        k = k_ref[pl.ds(pl.multiple_of((i - s) * t, t), t), :]
        for hh in range(2):
            u_sc[par, hh] = lax.dot_general(k, qm[hh], _NT, preferred_element_type=F32)

    def softplus(par, diagonal):
        for hh in range(2):
            logit = u_sc[par, hh]
            sp = _softplus2(logit)
            log_sig = logit - sp
            if diagonal:
                before = _row_iota((t, t)) < _lane_iota((t, t))
                sp = jnp.where(before, sp, 0.0)
                log_sig = jnp.where(before, log_sig, NEG_INF)
            ls_sc[par, hh] = log_sig
            hl_sc[par, hh] = sp.astype(BF16)

    def suffix_sums(par):
        tri = tri_ref[...]
        return [jnp.dot(tri, hl_sc[par, hh], preferred_element_type=F32) for hh in range(2)]

    def weights(sums, par):
        for hh in range(2):
            a_sc[par, hh] = jnp.exp2(ls_sc[par, hh] + sums[hh][:t]).astype(BF16)
            f_sc[par, hh] = jnp.exp2(c_sc[hh])
            c_sc[hh] = c_sc[hh] + sums[hh][t:t + 1]

    def values(s, par):
        vt = vt_ref[0, 0, jnp.minimum(i - s, i)]
        for hh in range(2):
            acc_sc[hh] = acc_sc[hh] + f_sc[par, hh] * jnp.dot(vt[hh * dh:(hh + 1) * dh], a_sc[par, hh],
                                                              preferred_element_type=F32)

    def iteration(s, par, prefetch):
        sums = suffix_sums(1 - par)
        values(s - 2, par)
        if prefetch:
            logits(jnp.minimum(s + 1, i), 1 - par)
        softplus(par, False)
        weights(sums, 1 - par)

    def drain(par):
        sums = suffix_sums(par)
        values(i - 1, 1 - par)
        weights(sums, par)
        values(i, par)

    logits(0, 0)
    logits(jnp.minimum(1, i), 1)
    softplus(0, True)

    def pair(s):
        iteration(s, 1, True)
        iteration(s + 1, 0, True)

    def quad_body(r, carry):
        pair(4 * r + 1)
        pair(4 * r + 3)
        return carry

    def pair_body(r, carry):
        pair(4 * (i // 4) + 2 * r + 1)
        return carry

    lax.fori_loop(0, i // 4, quad_body, 0)
    lax.fori_loop(0, (i % 4) // 2, pair_body, 0)

    @pl.when(i % 2 == 1)
    def _():
        iteration(i, 1, False)
        drain(1)

    @pl.when(i % 2 == 0)
    def _():
        drain(0)

    o_ref[...] = jnp.concatenate([acc_sc[0], acc_sc[1]], axis=0).T.astype(BF16)


def _stick_attention(proj, vt, batch, seq):
    t = D_T
    nq = seq // t
    nb = D_HEADS // 2
    tri = -np.concatenate([np.triu(np.ones((t, t), np.float32), 1),
                           np.ones((ONES_ROWS, t), np.float32)])
    return pl.pallas_call(
        _stick_body,
        grid=(batch, nb, nq),
        in_specs=[pl.BlockSpec((t, LANES), lambda b, h, i: (b * nq + i, h)),
                  pl.BlockSpec((seq, LANES), lambda b, h, i: (b, nb + h)),
                  pl.BlockSpec((1, 1, nq, LANES, t), lambda b, h, i: (b, h, 0, 0, 0)),
                  pl.BlockSpec((t + ONES_ROWS, t), lambda b, h, i: (0, 0))],
        out_specs=pl.BlockSpec((t, LANES), lambda b, h, i: (b * nq + i, h)),
        out_shape=jax.ShapeDtypeStruct((batch * seq, D_MODEL), BF16),
        scratch_shapes=[pltpu.VMEM((2, 1, t), F32), pltpu.VMEM((2, HEAD_DIM, t), F32),
                        pltpu.VMEM((2, 2, t, t), F32), pltpu.VMEM((2, 2, t, t), F32),
                        pltpu.VMEM((2, 2, t, t), BF16), pltpu.VMEM((2, 2, t, t), BF16),
                        pltpu.VMEM((2, 2, 1, t), F32)],
        compiler_params=_params(3),
        name="stick_attention",
    )(proj, proj, vt, jnp.asarray(tri, BF16))


def _dilated_body(slopes_ref, q0_ref, q1_ref, q2_ref, k0_ref, k1_ref, k2_ref, vt_ref, o_ref,
                  bias0_sc, bias1_sc, bias2_sc):
    tq, dv = C_T, C_V_DIM
    hp = pl.program_id(1)
    i = pl.program_id(2)
    q_start = i * tq
    q_refs = (q0_ref, q1_ref, q2_ref)
    k_refs = (k0_ref, k1_ref, k2_ref)
    bias_scs = (bias0_sc, bias1_sc, bias2_sc)
    seq = k0_ref.shape[0]
    spans = [min(-(-(w + tq) // tq) * tq, seq) for w, _ in C_PATTERNS]
    slopes2 = [[slopes_ref[g * C_HEADS + hp * 2 + hh] * LOG2E for g in range(len(C_PATTERNS))]
               for hh in range(2)]

    for g, (w, d) in enumerate(C_PATTERNS):
        @pl.when(i < spans[g] // tq)
        def _(g=g, w=w, d=d):
            span = spans[g]
            dist = q_start + _lane_iota((span, tq)) - _row_iota((span, tq))
            valid = (dist >= 0) & (dist <= w) & ((dist & (d - 1)) == 0)
            dist_f = dist.astype(F32)
            for hh in range(2):
                bias_scs[g][hh] = jnp.where(valid, -slopes2[hh][g] * dist_f, NEG_INF)

    ones = jnp.ones((ONES_ROWS, tq), BF16)
    for hh in range(2):
        qms = [_half_masked(q_refs[g][...].astype(F32) * (HEAD_DIM ** -0.5 * LOG2E))[hh]
               for g in range(len(C_PATTERNS))]
        starts = [jnp.maximum(q_start + tq - spans[g], 0) for g in range(len(C_PATTERNS))]
        raws = [lax.dot_general(k_refs[g][pl.ds(pl.multiple_of(starts[g], tq), spans[g]), :], qms[g], _NT,
                                preferred_element_type=F32) for g in range(len(C_PATTERNS))]
        outs, lses = [], []
        for g in range(len(C_PATTERNS)):
            span = spans[g]
            u = raws[g] + bias_scs[g][hh]
            m = jnp.max(u, axis=0, keepdims=True)
            e = jnp.exp2(u - m).astype(BF16)
            first_tile = starts[g] // tq
            acc = jnp.zeros((dv + ONES_ROWS, tq), F32)
            for r in range(span // tq):
                vaug = jnp.concatenate([vt_ref[0, hh, first_tile + r], ones], axis=0)
                acc = acc + jnp.dot(vaug, e[r * tq:(r + 1) * tq], preferred_element_type=F32)
            l = acc[dv:dv + 1]
            outs.append(acc[:dv] / l)
            lses.append(m + jnp.log(l) * LOG2E)
        mx = jnp.maximum(jnp.maximum(lses[0], lses[1]), lses[2])
        ws = [jnp.exp2(x - mx) for x in lses]
        mixed = (ws[0] * outs[0] + ws[1] * outs[1] + ws[2] * outs[2]) / (ws[0] + ws[1] + ws[2])
        o_ref[:, hh * dv:(hh + 1) * dv] = mixed.T.astype(BF16)


def _dilated_attention(proj, vt, batch, seq):
    tq = C_T
    nq = seq // tq
    zb = D_MODEL // LANES
    nqk = len(C_PATTERNS) * C_HEADS // 2
    spans = [min(-(-(w + tq) // tq) * tq, seq) for w, _ in C_PATTERNS]
    q_spec = lambda g: pl.BlockSpec((tq, LANES), lambda b, h, i: (b * nq + i, zb + 2 * g + h))
    k_spec = lambda g: pl.BlockSpec((seq, LANES), lambda b, h, i: (b, zb + nqk + 2 * g + h))
    return pl.pallas_call(
        _dilated_body,
        grid=(batch, 2, nq),
        in_specs=[_smem_spec(), q_spec(0), q_spec(1), q_spec(2), k_spec(0), k_spec(1), k_spec(2),
                  pl.BlockSpec((1, 2, nq, C_V_DIM, tq), lambda b, h, i: (b, h, 0, 0, 0))],
        out_specs=pl.BlockSpec((tq, 2 * C_V_DIM), lambda b, h, i: (b * nq + i, h)),
        out_shape=jax.ShapeDtypeStruct((batch * seq, D_MODEL), BF16),
        scratch_shapes=[pltpu.VMEM((2, sp, tq), F32) for sp in spans],
        compiler_params=_params(3),
        name="dilated_attention",
    )(jnp.asarray(_alibi_slopes(len(C_PATTERNS) * C_HEADS)), *([proj] * 6), vt)


def _gelu_tanh(x):
    return 0.5 * x * (1.0 + jnp.tanh(math.sqrt(2.0 / math.pi) * (x + 0.044715 * x * x * x)))


def _compress_body(ak_ref, av_ref, pos_ref, w1_ref, w2_ref, o_ref):
    n = ak_ref.shape[2]
    acc = jnp.zeros((n, LANES), F32)
    for kv, a_ref in enumerate((ak_ref, av_ref)):
        a = a_ref[0, 0].astype(F32)
        first = jnp.dot((a + pos_ref[kv, 0]).astype(BF16), w1_ref[kv, 0], preferred_element_type=F32)
        second = jnp.dot((a + pos_ref[kv, 1]).astype(BF16), w1_ref[kv, 1], preferred_element_type=F32)
        hidden = _gelu_tanh(first + pltpu.roll(second, n - 1, 0))
        acc = acc + jnp.dot(hidden.astype(BF16), w2_ref[kv], preferred_element_type=F32)
    o_ref[0, 0] = acc.astype(BF16)


def _compress(ak, av, pos, w1, w2):
    batch, hk, n, width = ak.shape
    a_spec = pl.BlockSpec((1, 1, n, width), lambda b, h: (b, h, 0, 0))
    return pl.pallas_call(
        _compress_body,
        grid=(batch, hk),
        in_specs=[a_spec, a_spec,
                  pl.BlockSpec(pos.shape, lambda b, h: (0, 0, 0, 0)),
                  pl.BlockSpec(w1.shape, lambda b, h: (0, 0, 0, 0)),
                  pl.BlockSpec(w2.shape, lambda b, h: (0, 0, 0))],
        out_specs=pl.BlockSpec((1, 1, n, LANES), lambda b, h: (b, h, 0, 0)),
        out_shape=jax.ShapeDtypeStruct((batch, hk, n, LANES), BF16),
        compiler_params=_params(2),
        name="nsa_compress",
    )(ak, av, pos, w1, w2)


def _nsa_body(slopes_ref, q_ref, sel_ref, vst_ref, win_ref, vwt_ref, cmp_ref, vct_ref, ovt_ref, kf_ref,
              oc_ref, os_ref, ow_ref, m_sc, acc_sc, qt_sc, neg_sc, u_sc, p_sc, a_sc):
    t, g_n, dh = A_T, A_GROUP, HEAD_DIM
    hk = pl.program_id(1)
    i = pl.program_id(2)
    q_start = i * t
    slopes2 = [slopes_ref[hk * g_n + g] * LOG2E for g in range(g_n)]

    lane = _lane_iota((1, LANES))
    t_lane = q_start + _lane_iota((1, t))

    q32 = q_ref[...].astype(F32) * (dh ** -0.5 * LOG2E)
    qpad = []
    for g in range(g_n):
        blk = q32[:, (g // 2) * LANES:(g // 2 + 1) * LANES]
        if g % 2:
            blk = pltpu.roll(blk, dh, 1)
        qpad.append(jnp.where(lane < dh, blk, 0.0).astype(BF16))

    qt = q32.T
    feat_row = _row_iota((A_FEAT_ROWS // 2, t))
    slope_rows = []
    for g in range(g_n):
        sr = jnp.zeros((A_FEAT_ROWS // 2, t), F32) + slopes2[g]
        hi = sr.astype(BF16).astype(F32)
        lo = (sr - hi).astype(BF16).astype(F32)
        slope_rows.append(jnp.where(feat_row == 0, hi, jnp.where(feat_row == 1, lo, 0.0)))
        for par in range(2):
            qt_sc[par, g, :dh] = qt[g * dh:(g + 1) * dh].astype(BF16)
            qt_sc[par, g, dh + A_FEAT_ROWS:] = jnp.zeros((LANES - dh - A_FEAT_ROWS, t), BF16)

    def store_token_major(ref, per_head):
        ref[...] = jnp.concatenate(per_head, axis=0).T.astype(BF16)

    n_cmp = cmp_ref.shape[2]
    ckv = cmp_ref[0, 0]
    vct = vct_ref[0, 0]
    cmp_end = A_CMP_STRIDE * _row_iota((n_cmp, t)) + (A_CMP_BLOCK - 1)
    cvalid = t_lane >= cmp_end
    cpos = (cmp_end - q_start).astype(F32)
    raw = [lax.dot_general(ckv, qpad[g], _NT, preferred_element_type=F32) for g in range(g_n)]
    probs = []
    for g in range(g_n):
        u = jnp.where(cvalid, raw[g] + slopes2[g] * cpos, NEG_INF)
        e = jnp.where(cvalid, jnp.exp2(u - jnp.max(u, axis=0, keepdims=True)), 0.0)
        probs.append(e * (1.0 / jnp.maximum(jnp.sum(e, axis=0, keepdims=True), TINY)))
    store_token_major(oc_ref, [jnp.dot(vct, p.astype(BF16), preferred_element_type=F32) for p in probs])
    p_sum = (probs[0] + probs[1]) + (probs[2] + probs[3])

    n_win = A_WINDOW // t + 1
    ones = jnp.ones((ONES_ROWS, t), BF16)
    rel = _row_iota((t, t)) - _lane_iota((t, t))
    feat_zero = jnp.zeros((A_FEAT_ROWS // 2, t), F32)
    qts = [jnp.concatenate([qt[g * dh:(g + 1) * dh], slope_rows[g], feat_zero,
                            jnp.zeros((LANES - dh - A_FEAT_ROWS, t), F32)], axis=0).astype(BF16) for g in range(g_n)]
    w_tiles, w_keep, w_shift = [], [], []
    for s in range(n_win):
        tile = jnp.maximum(i - s, 0)
        kv = win_ref[pl.ds(pl.multiple_of(tile * t, t), t), :]
        w_tiles.append((tile, jnp.where(lane < dh, kv, kf_ref[0])))
        low = jnp.where(i >= s, s * t - A_WINDOW, 2 * t)
        w_keep.append((rel > low) & (rel <= s * t))
        w_shift.append(float(-s * t))
    w_outs = []
    w_raw = [[jnp.dot(k_aug, qts[g], preferred_element_type=F32) for _, k_aug in w_tiles] for g in range(g_n)]
    for g in range(g_n):
        us = [jnp.where(w_keep[s], w_raw[g][s] + slopes2[g] * w_shift[s], NEG_INF) for s in range(n_win)]
        m = us[0].max(axis=0, keepdims=True)
        for s in range(1, n_win):
            m = jnp.maximum(m, us[s].max(axis=0, keepdims=True))
        acc = jnp.zeros((dh + ONES_ROWS, t), F32)
        for s in range(n_win):
            vaug = jnp.concatenate([vwt_ref[0, 0, w_tiles[s][0]], ones], axis=0)
            acc = acc + jnp.dot(vaug, jnp.exp2(us[s] - m).astype(BF16), preferred_element_type=F32)
        w_outs.append(acc[:dh] / jnp.maximum(acc[dh:dh + 1], TINY))
    store_token_major(ow_ref, w_outs)

    n_sel = ovt_ref.shape[0]
    hi, lo = _split_bf16(p_sum)
    ovt = ovt_ref[...]
    imp = jnp.dot(ovt, hi, preferred_element_type=F32) + jnp.dot(ovt, lo, preferred_element_type=F32)
    jrow = _row_iota((n_sel, t))
    cur = jnp.right_shift(t_lane, A_SEL_SHIFT)
    forced = (jrow == 0) | (jrow == cur) | (jrow == cur - 1)
    imp = jnp.where(jrow > cur, -1.0, imp + jnp.where(forced, A_FORCE_BONUS, 0.0))
    jrow_f = jrow.astype(F32)
    selected = jnp.zeros((n_sel, t), F32)
    for _ in range(min(A_SEL_TOPK, n_sel)):
        top = jnp.max(imp, axis=0, keepdims=True)
        first = jnp.min(jnp.where(imp == top, jrow_f, float(n_sel)), axis=0, keepdims=True)
        pick = jrow_f == first
        selected = jnp.where(pick, 1.0, selected)
        imp = jnp.where(pick, -2.0, imp)
    neg_sc[...] = jnp.where(selected > 0.5, 0.0, NEG_INF)

    blocks_per_pair = 2 * t // A_SEL_BLOCK
    m_sc[...] = jnp.full(m_sc.shape, NEG_INF, F32)
    acc_sc[...] = jnp.zeros(acc_sc.shape, F32)

    def score(s, par):
        tile = i - s
        kv = sel_ref[pl.ds(pl.multiple_of(tile * t, t), t), :]
        k_aug = jnp.where(lane < dh, kv, kf_ref[tile % 2])
        pair_start = pl.multiple_of((tile // 2) * blocks_per_pair, blocks_per_pair)
        mask_rows = neg_sc[pl.ds(pair_start, blocks_per_pair), :]
        for g in range(g_n):
            qt_sc[par, g, dh:dh + A_FEAT_ROWS] = jnp.concatenate([slope_rows[g], mask_rows], axis=0).astype(BF16)
            u_sc[par, g] = jnp.dot(k_aug, qt_sc[par, g], preferred_element_type=F32)

    def softmax(s, par, diagonal):
        offset = jnp.asarray(-s * t, jnp.int32).astype(F32)
        for g in range(g_n):
            cj = slopes2[g] * offset
            u = u_sc[par, g]
            if diagonal:
                u = jnp.where(rel <= 0, u, NEG_INF)
            m_old = m_sc[g]
            m_new = jnp.maximum(m_old, jnp.max(u, axis=0, keepdims=True) + cj)
            p_sc[par, g] = jnp.exp2(u - (m_new - cj)).astype(BF16)
            a_sc[par, g] = jnp.exp2(m_old - m_new)
            m_sc[g] = m_new

    def accumulate(s, par):
        vaug = jnp.concatenate([vst_ref[0, 0, i - s], ones], axis=0)
        pv = [jnp.dot(vaug, p_sc[par, g], preferred_element_type=F32) for g in range(g_n)]
        for g in range(g_n):
            acc_sc[g] = a_sc[par, g] * acc_sc[g] + pv[g]

    _pipeline3(i, score, softmax, accumulate)
    store_token_major(os_ref, [acc_sc[g, :dh] / jnp.maximum(acc_sc[g, dh:dh + 1], TINY) for g in range(g_n)])


def _nsa_attention(proj, vst, vwt, cmp_kv, batch, seq):
    t = A_T
    nq = seq // t
    hk_n, dh, g_n = A_KV_HEADS, HEAD_DIM, A_GROUP
    n_cmp = cmp_kv.shape[2]
    n_sel = seq // A_SEL_BLOCK
    ov = np.zeros((n_cmp, n_sel), np.float32)
    real = (seq - A_CMP_BLOCK) // A_CMP_STRIDE + 1
    cidx = A_CMP_STRIDE * np.arange(real)[:, None] + np.arange(A_CMP_BLOCK)[None, :]
    np.add.at(ov, (np.repeat(np.arange(real), A_CMP_BLOCK), (cidx // A_SEL_BLOCK).ravel()), 1.0 / A_CMP_BLOCK)
    qb = D_MODEL // (g_n * dh)
    sb = 2 * D_MODEL // LANES
    wb = sb + hk_n
    vct = cmp_kv[..., dh:].transpose(0, 1, 3, 2)
    kf = np.zeros((2, t, LANES), np.float32)
    kf[:, :, dh] = kf[:, :, dh + 1] = np.arange(t)
    for par in range(2):
        blk = par * (t // A_SEL_BLOCK) + np.arange(t) // A_SEL_BLOCK
        kf[par, np.arange(t), dh + A_FEAT_ROWS // 2 + blk] = 1.0
    vt_spec = pl.BlockSpec((1, 1, nq, dh, t), lambda b, h, i: (b, h, 0, 0, 0))
    o_spec = pl.BlockSpec((t, g_n * dh), lambda b, h, i: (b * nq + i, h))
    o_shape = jax.ShapeDtypeStruct((batch * seq, D_MODEL), BF16)
    return pl.pallas_call(
        _nsa_body,
        grid=(batch, hk_n, nq),
        in_specs=[_smem_spec(),
                  pl.BlockSpec((t, g_n * dh), lambda b, h, i: (b * nq + i, qb + h)),
                  pl.BlockSpec((seq, LANES), lambda b, h, i: (b, sb + h)),
                  vt_spec,
                  pl.BlockSpec((seq, LANES), lambda b, h, i: (b, wb + h)),
                  vt_spec,
                  pl.BlockSpec((1, 1, n_cmp, LANES), lambda b, h, i: (b, h, 0, 0)),
                  pl.BlockSpec((1, 1, dh, n_cmp), lambda b, h, i: (b, h, 0, 0)),
                  pl.BlockSpec((n_sel, n_cmp), lambda b, h, i: (0, 0)),
                  pl.BlockSpec((2, t, LANES), lambda b, h, i: (0, 0, 0))],
        out_specs=[o_spec, o_spec, o_spec],
        out_shape=[o_shape, o_shape, o_shape],
        scratch_shapes=[pltpu.VMEM((g_n, 1, t), F32),
                        pltpu.VMEM((g_n, dh + ONES_ROWS, t), F32),
                        pltpu.VMEM((2, g_n, LANES, t), BF16),
                        pltpu.VMEM((n_sel, t), F32),
                        pltpu.VMEM((2, g_n, t, t), F32),
                        pltpu.VMEM((2, g_n, t, t), BF16),
                        pltpu.VMEM((2, g_n, 1, t), F32)],
        compiler_params=_params(3),
        name="nsa_attention",
    )(jnp.asarray(_alibi_slopes(A_HEADS)), proj, proj, vst, proj, vwt, cmp_kv, vct, jnp.asarray(ov.T, BF16), jnp.asarray(kf, BF16))


def _nsa_layer(h, batch, seq, gain_pre, gain_post, w_in, w_out, pos_k, pos_v, w1_k, w2_k, w1_v, w2_v):
    dm, dh, hk_n = D_MODEL, HEAD_DIM, A_KV_HEADS
    kvw = hk_n * dh
    q0, kc0, vc0, ks0, vs0, kw0, vw0, gl0, z0 = np.cumsum((0, dm) + (kvw,) * 6 + (3 * A_HEADS,))
    cols = [np.arange(z0, z0 + dm), np.arange(q0, q0 + dm)]
    for k0, v0 in ((ks0, vs0), (kw0, vw0)):
        for hh in range(hk_n):
            cols += [np.arange(k0 + hh * dh, k0 + (hh + 1) * dh), np.arange(v0 + hh * dh, v0 + (hh + 1) * dh)]
    cols += [np.arange(kc0, kc0 + kvw), np.arange(vc0, vc0 + kvw), np.arange(gl0, gl0 + 3 * A_HEADS)]
    cols = np.concatenate(cols)
    w = jnp.pad(w_in[:, cols], ((0, 0), (0, LANES - 3 * A_HEADS))).astype(BF16)
    packed = lambda c0: [(c0 + hh * LANES, LANES, dh, dh) for hh in range(hk_n)]
    proj, vst, vwt = _proj_in(h, gain_pre, w, batch, [packed(2 * dm), packed(2 * dm + 2 * kvw)])

    grp = A_CMP_STRIDE
    cmp_col = 2 * dm + 4 * kvw

    def groups(c0):
        a = proj[:, c0:c0 + kvw].reshape(batch, seq // grp, grp, hk_n, dh)
        return a.transpose(0, 3, 1, 2, 4).reshape(batch, hk_n, seq // grp, grp * dh)

    half = grp * dh
    pos = jnp.stack([pos_k.reshape(2, 1, half), pos_v.reshape(2, 1, half)])
    w1 = jnp.stack([w1_k.reshape(2, half, A_CMP_HIDDEN), w1_v.reshape(2, half, A_CMP_HIDDEN)]).astype(BF16)
    w2 = jnp.stack([jnp.pad(w2_k, ((0, 0), (0, dh))), jnp.pad(w2_v, ((0, 0), (dh, 0)))]).astype(BF16)
    cmp_kv = _compress(groups(cmp_col), groups(cmp_col + kvw), pos, w1, w2)

    oc, osel, ow = _nsa_attention(proj, vst, vwt, cmp_kv, batch, seq)
    gl_blk = (cmp_col + 2 * kvw) // LANES
    return _proj_out([oc, osel, ow], proj, 0, w_out.astype(BF16), gain_post, h, gl_blk=gl_blk)


def kernel(x, norm_pre, norm_post, a_w_in, a_w_out, a_cmp_pos_k, a_cmp_pos_v, a_cmp_w1_k, a_cmp_w2_k,
           a_cmp_w1_v, a_cmp_w2_v, b_w_in, b_w_out, b_lambda, b_sub_gain, c_w_in, c_w_out, d_w_in, d_w_out):
    batch, seq, dm = x.shape
    h = x.reshape(batch * seq, dm)
    for i in range(DEPTH):
        mixer, j = i % 4, i // 4
        if mixer == 0:
            h = _nsa_layer(h, batch, seq, norm_pre[i], norm_post[i], a_w_in[j], a_w_out[j],
                           a_cmp_pos_k[j], a_cmp_pos_v[j], a_cmp_w1_k[j], a_cmp_w2_k[j],
                           a_cmp_w1_v[j], a_cmp_w2_v[j])
            continue
        if mixer == 1:
            heads = [(2 * dm + hh * LANES, LANES, 0, LANES) for hh in range(B_HEADS)]
            proj, vt = _proj_in(h, norm_pre[i], b_w_in[j].astype(BF16), batch, [heads])
            lambda_init = 0.8 - 0.6 * math.exp(-0.3 * i)
            o = _diff_attention(proj, vt, b_lambda[j], b_sub_gain[j], lambda_init, batch, seq)
            z_blk, w_out = 3, b_w_out[j]
        elif mixer == 2:
            qk = 2 * len(C_PATTERNS) * C_HEADS * HEAD_DIM
            w = jnp.concatenate([c_w_in[j][:, qk + dm:], c_w_in[j][:, :qk + dm]], axis=1)
            heads = [(dm + qk + hh * C_V_DIM, C_V_DIM, 0, C_V_DIM) for hh in range(C_HEADS)]
            proj, vt = _proj_in(h, norm_pre[i], w.astype(BF16), batch, [heads])
            o = _dilated_attention(proj, vt, batch, seq)
            z_blk, w_out = 0, c_w_out[j]
        else:
            heads = [(2 * dm + hh * LANES, LANES, 0, LANES) for hh in range(D_HEADS // 2)]
            proj, vt = _proj_in(h, norm_pre[i], d_w_in[j].astype(BF16), batch, [heads])
            o = _stick_attention(proj, vt, batch, seq)
            z_blk, w_out = 3, d_w_out[j]
        h = _proj_out([o], proj, z_blk, w_out.astype(BF16), norm_post[i], h)
    return h.reshape(batch, seq, dm)
```

```python
import functools
import math

import jax
import jax.numpy as jnp
import numpy as np
from jax import lax
from jax.experimental import pallas as pl
from jax.experimental.pallas import tpu as pltpu

F32 = jnp.float32
BF16 = jnp.bfloat16

D_MODEL = 1024
HEAD_DIM = 64
DEPTH = 4
RMS_EPS = 1e-6
NEG_INF = -1e30
TINY = 1e-30
LANES = 128
ONES_ROWS = 16
LOG2E = math.log2(math.e)
VMEM_LIMIT = 48 * 1024 * 1024

A_HEADS = 16
A_KV_HEADS = 4
A_GROUP = 4
A_CMP_BLOCK = 32
A_CMP_STRIDE = 16
A_CMP_HIDDEN = 256
A_SEL_BLOCK = 64
A_SEL_SHIFT = 6
A_SEL_TOPK = 16
A_WINDOW = 512
A_FORCE_BONUS = 1e3
A_T = 256
A_FEAT_ROWS = 16
B_HEADS = 8
B_T = 256
B_PER_STEP = 4
C_PATTERNS = ((128, 1), (512, 4), (2048, 16))
C_HEADS = 4
C_V_DIM = 256
C_T = 256
D_HEADS = 16
D_T = 256

PROJ_TM = 512
VT_TILE = 256
PROJ_CHUNK = 512

_NT = (((1,), (1,)), ((), ()))


def _alibi_slopes(n):
    return np.array([2.0 ** (-8.0 * (i + 1) / n) for i in range(n)], np.float32)


def _params(n_grid):
    return pltpu.CompilerParams(dimension_semantics=("arbitrary",) * n_grid,
                                vmem_limit_bytes=VMEM_LIMIT)


def _split_bf16(x):
    hi = x.astype(BF16)
    lo = (x - hi.astype(F32)).astype(BF16)
    return hi, lo


def _sigmoid(x):
    return 1.0 / (1.0 + jnp.exp(-x))


def _proj_in_body(h_ref, g_ref, w_ref, o_ref, *vt_refs, plans):
    x = h_ref[...]
    ms = jnp.mean(x * x, axis=-1, keepdims=True)
    u = (x * lax.rsqrt(ms + RMS_EPS) * g_ref[...]).astype(BF16)
    n = o_ref.shape[1]
    for c in range(0, n, PROJ_CHUNK):
        e = min(c + PROJ_CHUNK, n)
        y = jnp.dot(u, w_ref[:, c:e], preferred_element_type=F32)
        o_ref[:, c:e] = y.astype(BF16)
        for vt_ref, heads in zip(vt_refs, plans):
            for hd, (col, width, row, rows) in enumerate(heads):
                if c <= col and col + width <= e:
                    for r in range(PROJ_TM // VT_TILE):
                        tile = y[r * VT_TILE:(r + 1) * VT_TILE, col - c:col - c + width].T
                        vt_ref[0, hd, r] = tile[row:row + rows].astype(BF16)


def _proj_in(h, gain, w, batch, plans=()):
    m, n = h.shape[0], w.shape[1]
    steps_per_batch = m // batch // PROJ_TM
    vt_shapes = [jax.ShapeDtypeStruct((batch, len(p), m // batch // VT_TILE, p[0][3], VT_TILE), BF16) for p in plans]
    vt_specs = [pl.BlockSpec((1, len(p), PROJ_TM // VT_TILE, p[0][3], VT_TILE),
                             lambda i: (i // steps_per_batch, 0, i % steps_per_batch, 0, 0)) for p in plans]
    outs = pl.pallas_call(
        functools.partial(_proj_in_body, plans=tuple(plans)),
        grid=(m // PROJ_TM,),
        in_specs=[pl.BlockSpec((PROJ_TM, D_MODEL), lambda i: (i, 0)),
                  pl.BlockSpec((1, D_MODEL), lambda i: (0, 0)),
                  pl.BlockSpec((D_MODEL, n), lambda i: (0, 0))],
        out_specs=[pl.BlockSpec((PROJ_TM, n), lambda i: (i, 0))] + vt_specs,
        out_shape=[jax.ShapeDtypeStruct((m, n), BF16)] + vt_shapes,
        compiler_params=_params(1),
        name="proj_in",
    )(h, gain.reshape(1, D_MODEL), w)
    return outs


def _finish(o, z_ref, w_ref, g_ref, h_ref, out_ref):
    z = z_ref[...].astype(F32)
    gated = (o * (z * _sigmoid(z))).astype(BF16)
    y = jnp.dot(gated, w_ref[...], preferred_element_type=F32)
    ms = jnp.mean(y * y, axis=-1, keepdims=True)
    out_ref[...] = h_ref[...] + y * lax.rsqrt(ms + RMS_EPS) * g_ref[...]


def _proj_out_body(o_ref, z_ref, w_ref, g_ref, h_ref, out_ref):
    _finish(o_ref[...].astype(F32), z_ref, w_ref, g_ref, h_ref, out_ref)


def _proj_out_nsa_body(oc_ref, os_ref, ow_ref, gl_ref, eg_ref, z_ref, w_ref, g_ref, h_ref, out_ref):
    hi, lo = _split_bf16(_sigmoid(gl_ref[...].astype(F32)))
    eg = eg_ref[...]
    gates = jnp.dot(hi, eg, preferred_element_type=F32) + jnp.dot(lo, eg, preferred_element_type=F32)
    o = (gates[:, :D_MODEL] * oc_ref[...].astype(F32)
         + gates[:, D_MODEL:2 * D_MODEL] * os_ref[...].astype(F32)
         + gates[:, 2 * D_MODEL:] * ow_ref[...].astype(F32))
    _finish(o, z_ref, w_ref, g_ref, h_ref, out_ref)


def _proj_out(o_list, proj, z_blk, w_out, gain, h, gl_blk=None):
    m = h.shape[0]
    tm = PROJ_TM
    row = lambda i: (i, 0)
    fixed = lambda i: (0, 0)
    o_specs = [pl.BlockSpec((tm, D_MODEL), row) for _ in o_list]
    tail_specs = [pl.BlockSpec((tm, D_MODEL), lambda i: (i, z_blk)),
                  pl.BlockSpec((D_MODEL, D_MODEL), fixed),
                  pl.BlockSpec((1, D_MODEL), fixed),
                  pl.BlockSpec((tm, D_MODEL), row)]
    tail = [proj, w_out, gain.reshape(1, D_MODEL), h]
    if gl_blk is None:
        body, in_specs, args = _proj_out_body, o_specs + tail_specs, list(o_list) + tail
    else:
        eg = np.zeros((LANES, 3 * D_MODEL), np.float32)
        for br in range(3):
            for hd in range(A_HEADS):
                c0 = br * D_MODEL + hd * HEAD_DIM
                eg[br * A_HEADS + hd, c0:c0 + HEAD_DIM] = 1.0
        body = _proj_out_nsa_body
        in_specs = o_specs + [pl.BlockSpec((tm, LANES), lambda i: (i, gl_blk)),
                              pl.BlockSpec((LANES, 3 * D_MODEL), fixed)] + tail_specs
        args = list(o_list) + [proj, jnp.asarray(eg, BF16)] + tail
    return pl.pallas_call(
        body,
        grid=(m // tm,),
        in_specs=in_specs,
        out_specs=pl.BlockSpec((tm, D_MODEL), row),
        out_shape=jax.ShapeDtypeStruct((m, D_MODEL), F32),
        compiler_params=_params(1),
        name="proj_out",
    )(*args)


def _half_masked(q32):
    lane = lax.broadcasted_iota(jnp.int32, (1, LANES), 1)
    lo = jnp.where(lane < HEAD_DIM, q32, 0.0).astype(BF16)
    hi = jnp.where(lane >= HEAD_DIM, q32, 0.0).astype(BF16)
    return lo, hi


def _smem_spec():
    return pl.BlockSpec(memory_space=pltpu.SMEM)


def _row_iota(shape):
    return lax.broadcasted_iota(jnp.int32, shape, 0)


def _lane_iota(shape):
    return lax.broadcasted_iota(jnp.int32, shape, 1)


def _pipeline3(i, score, softmax, accumulate):
    score(0, 0)
    score(jnp.minimum(1, i), 1)
    softmax(0, 0, True)

    def pair(s):
        accumulate(s - 1, 0)
        score(s + 1, 0)
        softmax(s, 1, False)
        accumulate(s, 1)
        score(jnp.minimum(s + 2, i), 1)
        softmax(s + 1, 0, False)

    def quad_body(r, carry):
        pair(4 * r + 1)
        pair(4 * r + 3)
        return carry

    def pair_body(r, carry):
        pair(4 * (i // 4) + 2 * r + 1)
        return carry

    lax.fori_loop(0, i // 4, quad_body, 0)
    lax.fori_loop(0, (i % 4) // 2, pair_body, 0)

    @pl.when(i % 2 == 1)
    def _():
        accumulate(i - 1, 0)
        softmax(i, 1, False)
        accumulate(i, 1)

    @pl.when(i % 2 == 0)
    def _():
        accumulate(i, 0)


def _diff_body(slopes_ref, q_ref, k_ref, vt_ref, lam_ref, sg_ref, o_ref,
               m_sc, acc_sc, b0_sc, u_sc, p_sc, a_sc, *, lambda_init):
    t = B_T
    dv = 2 * HEAD_DIM
    hd = pl.program_id(1)
    i = pl.program_id(2)
    slopes2 = [slopes_ref[B_PER_STEP * hd + hh] * LOG2E for hh in range(B_PER_STEP)]
    chains = [(hh, mp) for hh in range(B_PER_STEP) for mp in range(2)]

    @pl.when(i == 0)
    def _():
        rows = _row_iota((t, t)).astype(F32)
        for hh in range(B_PER_STEP):
            b0_sc[hh] = slopes2[hh] * rows

    q32 = q_ref[...].astype(F32) * (HEAD_DIM ** -0.5 * LOG2E)
    qm = [_half_masked(q32[:, hh * LANES:(hh + 1) * LANES])[mp] for hh, mp in chains]
    m_sc[...] = jnp.full(m_sc.shape, NEG_INF, F32)
    acc_sc[...] = jnp.zeros(acc_sc.shape, F32)
    ones = jnp.ones((ONES_ROWS, t), BF16)

    def score(s, par):
        k = k_ref[pl.ds(pl.multiple_of((i - s) * t, t), t), :]
        for c, (hh, mp) in enumerate(chains):
            u_sc[par, c] = lax.dot_general(k[:, hh * LANES:(hh + 1) * LANES], qm[c], _NT,
                                           preferred_element_type=F32)

    def softmax(s, par, diagonal):
        offset = jnp.asarray(-s * t, jnp.int32).astype(F32)
        for c, (hh, mp) in enumerate(chains):
            cj = slopes2[hh] * offset
            u = u_sc[par, c] + b0_sc[hh]
            if diagonal:
                u = jnp.where(_row_iota((t, t)) <= _lane_iota((t, t)), u, NEG_INF)
            m_old = m_sc[c]
            m_new = jnp.maximum(m_old, jnp.max(u, axis=0, keepdims=True) + cj)
            p_sc[par, c] = jnp.exp2(u - (m_new - cj)).astype(BF16)
            a_sc[par, c] = jnp.exp2(m_old - m_new)
            m_sc[c] = m_new

    def accumulate(s, par):
        vaug = [jnp.concatenate([vt_ref[0, hh, i - s], ones], axis=0) for hh in range(B_PER_STEP)]
        pv = [jnp.dot(vaug[hh], p_sc[par, c], preferred_element_type=F32) for c, (hh, mp) in enumerate(chains)]
        for c in range(len(chains)):
            acc_sc[c] = a_sc[par, c] * acc_sc[c] + pv[c]

    _pipeline3(i, score, softmax, accumulate)

    lam = lam_ref[...]
    lam_full = (jnp.exp(jnp.sum(lam[0:1] * lam[1:2], axis=1, keepdims=True))
                - jnp.exp(jnp.sum(lam[2:3] * lam[3:4], axis=1, keepdims=True)) + lambda_init)
    for hh in range(B_PER_STEP):
        o0 = acc_sc[2 * hh, :dv] / jnp.maximum(acc_sc[2 * hh, dv:dv + 1], TINY)
        o1 = acc_sc[2 * hh + 1, :dv] / jnp.maximum(acc_sc[2 * hh + 1, dv:dv + 1], TINY)
        a = o0 - lam_full * o1
        ms = jnp.mean(a * a, axis=0, keepdims=True)
        y = (a * lax.rsqrt(ms + RMS_EPS) * sg_ref[...]) * (1.0 - lambda_init)
        o_ref[:, hh * dv:(hh + 1) * dv] = y.T.astype(BF16)


def _diff_attention(proj, vt, lam, sub_gain, lambda_init, batch, seq):
    t = B_T
    nq = seq // t
    per = B_PER_STEP
    ng = B_HEADS // per
    dv = 2 * HEAD_DIM
    n_chain = 2 * per
    return pl.pallas_call(
        functools.partial(_diff_body, lambda_init=lambda_init),
        grid=(batch, ng, nq),
        in_specs=[_smem_spec(),
                  pl.BlockSpec((t, per * dv), lambda b, h, i: (b * nq + i, h)),
                  pl.BlockSpec((seq, per * dv), lambda b, h, i: (b, ng + h)),
                  pl.BlockSpec((1, per, nq, dv, t), lambda b, h, i: (b, h, 0, 0, 0)),
                  pl.BlockSpec((4, HEAD_DIM), lambda b, h, i: (0, 0)),
                  pl.BlockSpec((dv, 1), lambda b, h, i: (0, 0))],
        out_specs=pl.BlockSpec((t, per * dv), lambda b, h, i: (b * nq + i, h)),
        out_shape=jax.ShapeDtypeStruct((batch * seq, D_MODEL), BF16),
        scratch_shapes=[pltpu.VMEM((n_chain, 1, t), F32), pltpu.VMEM((n_chain, dv + ONES_ROWS, t), F32),
                        pltpu.VMEM((per, t, t), F32),
                        pltpu.VMEM((2, n_chain, t, t), F32), pltpu.VMEM((2, n_chain, t, t), BF16),
                        pltpu.VMEM((2, n_chain, 1, t), F32)],
        compiler_params=_params(3),
        name="diff_attention",
    )(jnp.asarray(_alibi_slopes(B_HEADS)), proj, proj, vt, lam, sub_gain.reshape(dv, 1))


def _softplus2(x):
    sign = jnp.uint32(0x80000000)
    neg_abs = lax.bitcast_convert_type(lax.bitcast_convert_type(x, jnp.uint32) | sign, F32)
    return jnp.maximum(x, 0.0) + jnp.log(1.0 + jnp.exp2(neg_abs)) * LOG2E


def _stick_body(q_ref, k_ref, vt_ref, tri_ref, o_ref, c_sc, acc_sc, u_sc, ls_sc, hl_sc, a_sc, f_sc):
    t = D_T
    dh = HEAD_DIM
    i = pl.program_id(2)
    qm = _half_masked(q_ref[...].astype(F32) * (dh ** -0.5 * LOG2E))
    c_sc[...] = jnp.zeros(c_sc.shape, F32)
    acc_sc[...] = jnp.zeros(acc_sc.shape, F32)
    a_sc[...] = jnp.zeros(a_sc.shape, BF16)
    f_sc[...] = jnp.zeros(f_sc.shape, F32)

    def logits(s, par):
        k = k_ref[pl.ds(pl.multiple_of((i - s) * t, t), t), :]
        for hh in range(2):
            u_sc[par, hh] = lax.dot_general(k, qm[hh], _NT, preferred_element_type=F32)

    def softplus(par, diagonal):
        for hh in range(2):
            logit = u_sc[par, hh]
            sp = _softplus2(logit)
            log_sig = logit - sp
            if diagonal:
                before = _row_iota((t, t)) < _lane_iota((t, t))
                sp = jnp.where(before, sp, 0.0)
                log_sig = jnp.where(before, log_sig, NEG_INF)
            ls_sc[par, hh] = log_sig
            hl_sc[par, hh] = sp.astype(BF16)

    def suffix_sums(par):
        tri = tri_ref[...]
        return [jnp.dot(tri, hl_sc[par, hh], preferred_element_type=F32) for hh in range(2)]

    def weights(sums, par):
        for hh in range(2):
            a_sc[par, hh] = jnp.exp2(ls_sc[par, hh] + sums[hh][:t]).astype(BF16)
            f_sc[par, hh] = jnp.exp2(c_sc[hh])
            c_sc[hh] = c_sc[hh] + sums[hh][t:t + 1]

    def values(s, par):
        vt = vt_ref[0, 0, jnp.minimum(i - s, i)]
        for hh in range(2):
            acc_sc[hh] = acc_sc[hh] + f_sc[par, hh] * jnp.dot(vt[hh * dh:(hh + 1) * dh], a_sc[par, hh],
                                                              preferred_element_type=F32)

    def iteration(s, par, prefetch):
        sums = suffix_sums(1 - par)
        values(s - 2, par)
        if prefetch:
            logits(jnp.minimum(s + 1, i), 1 - par)
        softplus(par, False)
        weights(sums, 1 - par)

    def drain(par):
        sums = suffix_sums(par)
        values(i - 1, 1 - par)
        weights(sums, par)
        values(i, par)

    logits(0, 0)
    logits(jnp.minimum(1, i), 1)
    softplus(0, True)

    def pair(s):
        iteration(s, 1, True)
        iteration(s + 1, 0, True)

    def quad_body(r, carry):
        pair(4 * r + 1)
        pair(4 * r + 3)
        return carry

    def pair_body(r, carry):
        pair(4 * (i // 4) + 2 * r + 1)
        return carry

    lax.fori_loop(0, i // 4, quad_body, 0)
    lax.fori_loop(0, (i % 4) // 2, pair_body, 0)

    @pl.when(i % 2 == 1)
    def _():
        iteration(i, 1, False)
        drain(1)

    @pl.when(i % 2 == 0)
    def _():
        drain(0)

    o_ref[...] = jnp.concatenate([acc_sc[0], acc_sc[1]], axis=0).T.astype(BF16)


def _stick_attention(proj, vt, batch, seq):
    t = D_T
    nq = seq // t
    nb = D_HEADS // 2
    tri = -np.concatenate([np.triu(np.ones((t, t), np.float32), 1),
                           np.ones((ONES_ROWS, t), np.float32)])
    return pl.pallas_call(
        _stick_body,
        grid=(batch, nb, nq),
        in_specs=[pl.BlockSpec((t, LANES), lambda b, h, i: (b * nq + i, h)),
                  pl.BlockSpec((seq, LANES), lambda b, h, i: (b, nb + h)),
                  pl.BlockSpec((1, 1, nq, LANES, t), lambda b, h, i: (b, h, 0, 0, 0)),
                  pl.BlockSpec((t + ONES_ROWS, t), lambda b, h, i: (0, 0))],
        out_specs=pl.BlockSpec((t, LANES), lambda b, h, i: (b * nq + i, h)),
        out_shape=jax.ShapeDtypeStruct((batch * seq, D_MODEL), BF16),
        scratch_shapes=[pltpu.VMEM((2, 1, t), F32), pltpu.VMEM((2, HEAD_DIM, t), F32),
                        pltpu.VMEM((2, 2, t, t), F32), pltpu.VMEM((2, 2, t, t), F32),
                        pltpu.VMEM((2, 2, t, t), BF16), pltpu.VMEM((2, 2, t, t), BF16),
                        pltpu.VMEM((2, 2, 1, t), F32)],
        compiler_params=_params(3),
        name="stick_attention",
    )(proj, proj, vt, jnp.asarray(tri, BF16))


def _dilated_body(slopes_ref, q0_ref, q1_ref, q2_ref, k0_ref, k1_ref, k2_ref, vt_ref, o_ref,
                  bias0_sc, bias1_sc, bias2_sc):
    tq, dv = C_T, C_V_DIM
    hp = pl.program_id(1)
    i = pl.program_id(2)
    q_start = i * tq
    q_refs = (q0_ref, q1_ref, q2_ref)
    k_refs = (k0_ref, k1_ref, k2_ref)
    bias_scs = (bias0_sc, bias1_sc, bias2_sc)
    seq = k0_ref.shape[0]
    spans = [min(-(-(w + tq) // tq) * tq, seq) for w, _ in C_PATTERNS]
    slopes2 = [[slopes_ref[g * C_HEADS + hp * 2 + hh] * LOG2E for g in range(len(C_PATTERNS))]
               for hh in range(2)]

    for g, (w, d) in enumerate(C_PATTERNS):
        @pl.when(i < spans[g] // tq)
        def _(g=g, w=w, d=d):
            span = spans[g]
            dist = q_start + _lane_iota((span, tq)) - _row_iota((span, tq))
            valid = (dist >= 0) & (dist <= w) & ((dist & (d - 1)) == 0)
            dist_f = dist.astype(F32)
            for hh in range(2):
                bias_scs[g][hh] = jnp.where(valid, -slopes2[hh][g] * dist_f, NEG_INF)

    ones = jnp.ones((ONES_ROWS, tq), BF16)
    for hh in range(2):
        qms = [_half_masked(q_refs[g][...].astype(F32) * (HEAD_DIM ** -0.5 * LOG2E))[hh]
               for g in range(len(C_PATTERNS))]
        starts = [jnp.maximum(q_start + tq - spans[g], 0) for g in range(len(C_PATTERNS))]
        raws = [lax.dot_general(k_refs[g][pl.ds(pl.multiple_of(starts[g], tq), spans[g]), :], qms[g], _NT,
                                preferred_element_type=F32) for g in range(len(C_PATTERNS))]
        outs, lses = [], []
        for g in range(len(C_PATTERNS)):
            span = spans[g]
            u = raws[g] + bias_scs[g][hh]
            m = jnp.max(u, axis=0, keepdims=True)
            e = jnp.exp2(u - m).astype(BF16)
            first_tile = starts[g] // tq
            acc = jnp.zeros((dv + ONES_ROWS, tq), F32)
            for r in range(span // tq):
                vaug = jnp.concatenate([vt_ref[0, hh, first_tile + r], ones], axis=0)
                acc = acc + jnp.dot(vaug, e[r * tq:(r + 1) * tq], preferred_element_type=F32)
            l = acc[dv:dv + 1]
            outs.append(acc[:dv] / l)
            lses.append(m + jnp.log(l) * LOG2E)
        mx = jnp.maximum(jnp.maximum(lses[0], lses[1]), lses[2])
        ws = [jnp.exp2(x - mx) for x in lses]
        mixed = (ws[0] * outs[0] + ws[1] * outs[1] + ws[2] * outs[2]) / (ws[0] + ws[1] + ws[2])
        o_ref[:, hh * dv:(hh + 1) * dv] = mixed.T.astype(BF16)


def _dilated_attention(proj, vt, batch, seq):
    tq = C_T
    nq = seq // tq
    zb = D_MODEL // LANES
    nqk = len(C_PATTERNS) * C_HEADS // 2
    spans = [min(-(-(w + tq) // tq) * tq, seq) for w, _ in C_PATTERNS]
    q_spec = lambda g: pl.BlockSpec((tq, LANES), lambda b, h, i: (b * nq + i, zb + 2 * g + h))
    k_spec = lambda g: pl.BlockSpec((seq, LANES), lambda b, h, i: (b, zb + nqk + 2 * g + h))
    return pl.pallas_call(
        _dilated_body,
        grid=(batch, 2, nq),
        in_specs=[_smem_spec(), q_spec(0), q_spec(1), q_spec(2), k_spec(0), k_spec(1), k_spec(2),
                  pl.BlockSpec((1, 2, nq, C_V_DIM, tq), lambda b, h, i: (b, h, 0, 0, 0))],
        out_specs=pl.BlockSpec((tq, 2 * C_V_DIM), lambda b, h, i: (b * nq + i, h)),
        out_shape=jax.ShapeDtypeStruct((batch * seq, D_MODEL), BF16),
        scratch_shapes=[pltpu.VMEM((2, sp, tq), F32) for sp in spans],
        compiler_params=_params(3),
        name="dilated_attention",
    )(jnp.asarray(_alibi_slopes(len(C_PATTERNS) * C_HEADS)), *([proj] * 6), vt)


def _gelu_tanh(x):
    return 0.5 * x * (1.0 + jnp.tanh(math.sqrt(2.0 / math.pi) * (x + 0.044715 * x * x * x)))


def _compress_body(ak_ref, av_ref, pos_ref, w1_ref, w2_ref, o_ref):
    n = ak_ref.shape[2]
    acc = jnp.zeros((n, LANES), F32)
    for kv, a_ref in enumerate((ak_ref, av_ref)):
        a = a_ref[0, 0].astype(F32)
        first = jnp.dot((a + pos_ref[kv, 0]).astype(BF16), w1_ref[kv, 0], preferred_element_type=F32)
        second = jnp.dot((a + pos_ref[kv, 1]).astype(BF16), w1_ref[kv, 1], preferred_element_type=F32)
        hidden = _gelu_tanh(first + pltpu.roll(second, n - 1, 0))
        acc = acc + jnp.dot(hidden.astype(BF16), w2_ref[kv], preferred_element_type=F32)
    o_ref[0, 0] = acc.astype(BF16)


def _compress(ak, av, pos, w1, w2):
    batch, hk, n, width = ak.shape
    a_spec = pl.BlockSpec((1, 1, n, width), lambda b, h: (b, h, 0, 0))
    return pl.pallas_call(
        _compress_body,
        grid=(batch, hk),
        in_specs=[a_spec, a_spec,
                  pl.BlockSpec(pos.shape, lambda b, h: (0, 0, 0, 0)),
                  pl.BlockSpec(w1.shape, lambda b, h: (0, 0, 0, 0)),
                  pl.BlockSpec(w2.shape, lambda b, h: (0, 0, 0))],
        out_specs=pl.BlockSpec((1, 1, n, LANES), lambda b, h: (b, h, 0, 0)),
        out_shape=jax.ShapeDtypeStruct((batch, hk, n, LANES), BF16),
        compiler_params=_params(2),
        name="nsa_compress",
    )(ak, av, pos, w1, w2)


def _nsa_body(slopes_ref, q_ref, sel_ref, vst_ref, win_ref, vwt_ref, cmp_ref, vct_ref, ovt_ref, kf_ref,
              oc_ref, os_ref, ow_ref, m_sc, acc_sc, qt_sc, neg_sc, u_sc, p_sc, a_sc):
    t, g_n, dh = A_T, A_GROUP, HEAD_DIM
    hk = pl.program_id(1)
    i = pl.program_id(2)
    q_start = i * t
    slopes2 = [slopes_ref[hk * g_n + g] * LOG2E for g in range(g_n)]

    lane = _lane_iota((1, LANES))
    t_lane = q_start + _lane_iota((1, t))

    q32 = q_ref[...].astype(F32) * (dh ** -0.5 * LOG2E)
    qpad = []
    for g in range(g_n):
        blk = q32[:, (g // 2) * LANES:(g // 2 + 1) * LANES]
        if g % 2:
            blk = pltpu.roll(blk, dh, 1)
        qpad.append(jnp.where(lane < dh, blk, 0.0).astype(BF16))

    qt = q32.T
    feat_row = _row_iota((A_FEAT_ROWS // 2, t))
    slope_rows = []
    for g in range(g_n):
        sr = jnp.zeros((A_FEAT_ROWS // 2, t), F32) + slopes2[g]
        hi = sr.astype(BF16).astype(F32)
        lo = (sr - hi).astype(BF16).astype(F32)
        slope_rows.append(jnp.where(feat_row == 0, hi, jnp.where(feat_row == 1, lo, 0.0)))
        for par in range(2):
            qt_sc[par, g, :dh] = qt[g * dh:(g + 1) * dh].astype(BF16)
            qt_sc[par, g, dh + A_FEAT_ROWS:] = jnp.zeros((LANES - dh - A_FEAT_ROWS, t), BF16)

    def store_token_major(ref, per_head):
        ref[...] = jnp.concatenate(per_head, axis=0).T.astype(BF16)

    n_cmp = cmp_ref.shape[2]
    ckv = cmp_ref[0, 0]
    vct = vct_ref[0, 0]
    cmp_end = A_CMP_STRIDE * _row_iota((n_cmp, t)) + (A_CMP_BLOCK - 1)
    cvalid = t_lane >= cmp_end
    cpos = (cmp_end - q_start).astype(F32)
    raw = [lax.dot_general(ckv, qpad[g], _NT, preferred_element_type=F32) for g in range(g_n)]
    probs = []
    for g in range(g_n):
        u = jnp.where(cvalid, raw[g] + slopes2[g] * cpos, NEG_INF)
        e = jnp.where(cvalid, jnp.exp2(u - jnp.max(u, axis=0, keepdims=True)), 0.0)
        probs.append(e * (1.0 / jnp.maximum(jnp.sum(e, axis=0, keepdims=True), TINY)))
    store_token_major(oc_ref, [jnp.dot(vct, p.astype(BF16), preferred_element_type=F32) for p in probs])
    p_sum = (probs[0] + probs[1]) + (probs[2] + probs[3])

    n_win = A_WINDOW // t + 1
    ones = jnp.ones((ONES_ROWS, t), BF16)
    rel = _row_iota((t, t)) - _lane_iota((t, t))
    feat_zero = jnp.zeros((A_FEAT_ROWS // 2, t), F32)
    qts = [jnp.concatenate([qt[g * dh:(g + 1) * dh], slope_rows[g], feat_zero,
                            jnp.zeros((LANES - dh - A_FEAT_ROWS, t), F32)], axis=0).astype(BF16) for g in range(g_n)]
    w_tiles, w_keep, w_shift = [], [], []
    for s in range(n_win):
        tile = jnp.maximum(i - s, 0)
        kv = win_ref[pl.ds(pl.multiple_of(tile * t, t), t), :]
        w_tiles.append((tile, jnp.where(lane < dh, kv, kf_ref[0])))
        low = jnp.where(i >= s, s * t - A_WINDOW, 2 * t)
        w_keep.append((rel > low) & (rel <= s * t))
        w_shift.append(float(-s * t))
    w_outs = []
    w_raw = [[jnp.dot(k_aug, qts[g], preferred_element_type=F32) for _, k_aug in w_tiles] for g in range(g_n)]
    for g in range(g_n):
        us = [jnp.where(w_keep[s], w_raw[g][s] + slopes2[g] * w_shift[s], NEG_INF) for s in range(n_win)]
        m = us[0].max(axis=0, keepdims=True)
        for s in range(1, n_win):
            m = jnp.maximum(m, us[s].max(axis=0, keepdims=True))
        acc = jnp.zeros((dh + ONES_ROWS, t), F32)
        for s in range(n_win):
            vaug = jnp.concatenate([vwt_ref[0, 0, w_tiles[s][0]], ones], axis=0)
            acc = acc + jnp.dot(vaug, jnp.exp2(us[s] - m).astype(BF16), preferred_element_type=F32)
        w_outs.append(acc[:dh] / jnp.maximum(acc[dh:dh + 1], TINY))
    store_token_major(ow_ref, w_outs)

    n_sel = ovt_ref.shape[0]
    hi, lo = _split_bf16(p_sum)
    ovt = ovt_ref[...]
    imp = jnp.dot(ovt, hi, preferred_element_type=F32) + jnp.dot(ovt, lo, preferred_element_type=F32)
    jrow = _row_iota((n_sel, t))
    cur = jnp.right_shift(t_lane, A_SEL_SHIFT)
    forced = (jrow == 0) | (jrow == cur) | (jrow == cur - 1)
    imp = jnp.where(jrow > cur, -1.0, imp + jnp.where(forced, A_FORCE_BONUS, 0.0))
    jrow_f = jrow.astype(F32)
    selected = jnp.zeros((n_sel, t), F32)
    for _ in range(min(A_SEL_TOPK, n_sel)):
        top = jnp.max(imp, axis=0, keepdims=True)
        first = jnp.min(jnp.where(imp == top, jrow_f, float(n_sel)), axis=0, keepdims=True)
        pick = jrow_f == first
        selected = jnp.where(pick, 1.0, selected)
        imp = jnp.where(pick, -2.0, imp)
    neg_sc[...] = jnp.where(selected > 0.5, 0.0, NEG_INF)

    blocks_per_pair = 2 * t // A_SEL_BLOCK
    m_sc[...] = jnp.full(m_sc.shape, NEG_INF, F32)
    acc_sc[...] = jnp.zeros(acc_sc.shape, F32)

    def score(s, par):
        tile = i - s
        kv = sel_ref[pl.ds(pl.multiple_of(tile * t, t), t), :]
        k_aug = jnp.where(lane < dh, kv, kf_ref[tile % 2])
        pair_start = pl.multiple_of((tile // 2) * blocks_per_pair, blocks_per_pair)
        mask_rows = neg_sc[pl.ds(pair_start, blocks_per_pair), :]
        for g in range(g_n):
            qt_sc[par, g, dh:dh + A_FEAT_ROWS] = jnp.concatenate([slope_rows[g], mask_rows], axis=0).astype(BF16)
            u_sc[par, g] = jnp.dot(k_aug, qt_sc[par, g], preferred_element_type=F32)

    def softmax(s, par, diagonal):
        offset = jnp.asarray(-s * t, jnp.int32).astype(F32)
        for g in range(g_n):
            cj = slopes2[g] * offset
            u = u_sc[par, g]
            if diagonal:
                u = jnp.where(rel <= 0, u, NEG_INF)
            m_old = m_sc[g]
            m_new = jnp.maximum(m_old, jnp.max(u, axis=0, keepdims=True) + cj)
            p_sc[par, g] = jnp.exp2(u - (m_new - cj)).astype(BF16)
            a_sc[par, g] = jnp.exp2(m_old - m_new)
            m_sc[g] = m_new

    def accumulate(s, par):
        vaug = jnp.concatenate([vst_ref[0, 0, i - s], ones], axis=0)
        pv = [jnp.dot(vaug, p_sc[par, g], preferred_element_type=F32) for g in range(g_n)]
        for g in range(g_n):
            acc_sc[g] = a_sc[par, g] * acc_sc[g] + pv[g]

    _pipeline3(i, score, softmax, accumulate)
    store_token_major(os_ref, [acc_sc[g, :dh] / jnp.maximum(acc_sc[g, dh:dh + 1], TINY) for g in range(g_n)])


def _nsa_attention(proj, vst, vwt, cmp_kv, batch, seq):
    t = A_T
    nq = seq // t
    hk_n, dh, g_n = A_KV_HEADS, HEAD_DIM, A_GROUP
    n_cmp = cmp_kv.shape[2]
    n_sel = seq // A_SEL_BLOCK
    ov = np.zeros((n_cmp, n_sel), np.float32)
    real = (seq - A_CMP_BLOCK) // A_CMP_STRIDE + 1
    cidx = A_CMP_STRIDE * np.arange(real)[:, None] + np.arange(A_CMP_BLOCK)[None, :]
    np.add.at(ov, (np.repeat(np.arange(real), A_CMP_BLOCK), (cidx // A_SEL_BLOCK).ravel()), 1.0 / A_CMP_BLOCK)
    qb = D_MODEL // (g_n * dh)
    sb = 2 * D_MODEL // LANES
    wb = sb + hk_n
    vct = cmp_kv[..., dh:].transpose(0, 1, 3, 2)
    kf = np.zeros((2, t, LANES), np.float32)
    kf[:, :, dh] = kf[:, :, dh + 1] = np.arange(t)
    for par in range(2):
        blk = par * (t // A_SEL_BLOCK) + np.arange(t) // A_SEL_BLOCK
        kf[par, np.arange(t), dh + A_FEAT_ROWS // 2 + blk] = 1.0
    vt_spec = pl.BlockSpec((1, 1, nq, dh, t), lambda b, h, i: (b, h, 0, 0, 0))
    o_spec = pl.BlockSpec((t, g_n * dh), lambda b, h, i: (b * nq + i, h))
    o_shape = jax.ShapeDtypeStruct((batch * seq, D_MODEL), BF16)
    return pl.pallas_call(
        _nsa_body,
        grid=(batch, hk_n, nq),
        in_specs=[_smem_spec(),
                  pl.BlockSpec((t, g_n * dh), lambda b, h, i: (b * nq + i, qb + h)),
                  pl.BlockSpec((seq, LANES), lambda b, h, i: (b, sb + h)),
                  vt_spec,
                  pl.BlockSpec((seq, LANES), lambda b, h, i: (b, wb + h)),
                  vt_spec,
                  pl.BlockSpec((1, 1, n_cmp, LANES), lambda b, h, i: (b, h, 0, 0)),
                  pl.BlockSpec((1, 1, dh, n_cmp), lambda b, h, i: (b, h, 0, 0)),
                  pl.BlockSpec((n_sel, n_cmp), lambda b, h, i: (0, 0)),
                  pl.BlockSpec((2, t, LANES), lambda b, h, i: (0, 0, 0))],
        out_specs=[o_spec, o_spec, o_spec],
        out_shape=[o_shape, o_shape, o_shape],
        scratch_shapes=[pltpu.VMEM((g_n, 1, t), F32),
                        pltpu.VMEM((g_n, dh + ONES_ROWS, t), F32),
                        pltpu.VMEM((2, g_n, LANES, t), BF16),
                        pltpu.VMEM((n_sel, t), F32),
                        pltpu.VMEM((2, g_n, t, t), F32),
                        pltpu.VMEM((2, g_n, t, t), BF16),
                        pltpu.VMEM((2, g_n, 1, t), F32)],
        compiler_params=_params(3),
        name="nsa_attention",
    )(jnp.asarray(_alibi_slopes(A_HEADS)), proj, proj, vst, proj, vwt, cmp_kv, vct, jnp.asarray(ov.T, BF16), jnp.asarray(kf, BF16))


def _nsa_layer(h, batch, seq, gain_pre, gain_post, w_in, w_out, pos_k, pos_v, w1_k, w2_k, w1_v, w2_v):
    dm, dh, hk_n = D_MODEL, HEAD_DIM, A_KV_HEADS
    kvw = hk_n * dh
    q0, kc0, vc0, ks0, vs0, kw0, vw0, gl0, z0 = np.cumsum((0, dm) + (kvw,) * 6 + (3 * A_HEADS,))
    cols = [np.arange(z0, z0 + dm), np.arange(q0, q0 + dm)]
    for k0, v0 in ((ks0, vs0), (kw0, vw0)):
        for hh in range(hk_n):
            cols += [np.arange(k0 + hh * dh, k0 + (hh + 1) * dh), np.arange(v0 + hh * dh, v0 + (hh + 1) * dh)]
    cols += [np.arange(kc0, kc0 + kvw), np.arange(vc0, vc0 + kvw), np.arange(gl0, gl0 + 3 * A_HEADS)]
    cols = np.concatenate(cols)
    w = jnp.pad(w_in[:, cols], ((0, 0), (0, LANES - 3 * A_HEADS))).astype(BF16)
    packed = lambda c0: [(c0 + hh * LANES, LANES, dh, dh) for hh in range(hk_n)]
    proj, vst, vwt = _proj_in(h, gain_pre, w, batch, [packed(2 * dm), packed(2 * dm + 2 * kvw)])

    grp = A_CMP_STRIDE
    cmp_col = 2 * dm + 4 * kvw

    def groups(c0):
        a = proj[:, c0:c0 + kvw].reshape(batch, seq // grp, grp, hk_n, dh)
        return a.transpose(0, 3, 1, 2, 4).reshape(batch, hk_n, seq // grp, grp * dh)

    half = grp * dh
    pos = jnp.stack([pos_k.reshape(2, 1, half), pos_v.reshape(2, 1, half)])
    w1 = jnp.stack([w1_k.reshape(2, half, A_CMP_HIDDEN), w1_v.reshape(2, half, A_CMP_HIDDEN)]).astype(BF16)
    w2 = jnp.stack([jnp.pad(w2_k, ((0, 0), (0, dh))), jnp.pad(w2_v, ((0, 0), (dh, 0)))]).astype(BF16)
    cmp_kv = _compress(groups(cmp_col), groups(cmp_col + kvw), pos, w1, w2)

    oc, osel, ow = _nsa_attention(proj, vst, vwt, cmp_kv, batch, seq)
    gl_blk = (cmp_col + 2 * kvw) // LANES
    return _proj_out([oc, osel, ow], proj, 0, w_out.astype(BF16), gain_post, h, gl_blk=gl_blk)


def kernel(x, norm_pre, norm_post, a_w_in, a_w_out, a_cmp_pos_k, a_cmp_pos_v, a_cmp_w1_k, a_cmp_w2_k,
           a_cmp_w1_v, a_cmp_w2_v, b_w_in, b_w_out, b_lambda, b_sub_gain, c_w_in, c_w_out, d_w_in, d_w_out):
    batch, seq, dm = x.shape
    h = x.reshape(batch * seq, dm)
    for i in range(DEPTH):
        mixer, j = i % 4, i // 4
        if mixer == 0:
            h = _nsa_layer(h, batch, seq, norm_pre[i], norm_post[i], a_w_in[j], a_w_out[j],
                           a_cmp_pos_k[j], a_cmp_pos_v[j], a_cmp_w1_k[j], a_cmp_w2_k[j],
                           a_cmp_w1_v[j], a_cmp_w2_v[j])
            continue
        if mixer == 1:
            heads = [(2 * dm + hh * LANES, LANES, 0, LANES) for hh in range(B_HEADS)]
            proj, vt = _proj_in(h, norm_pre[i], b_w_in[j].astype(BF16), batch, [heads])
            lambda_init = 0.8 - 0.6 * math.exp(-0.3 * i)
            o = _diff_attention(proj, vt, b_lambda[j], b_sub_gain[j], lambda_init, batch, seq)
            z_blk, w_out = 3, b_w_out[j]
        elif mixer == 2:
            qk = 2 * len(C_PATTERNS) * C_HEADS * HEAD_DIM
            w = jnp.concatenate([c_w_in[j][:, qk + dm:], c_w_in[j][:, :qk + dm]], axis=1)
            heads = [(dm + qk + hh * C_V_DIM, C_V_DIM, 0, C_V_DIM) for hh in range(C_HEADS)]
            proj, vt = _proj_in(h, norm_pre[i], w.astype(BF16), batch, [heads])
            o = _dilated_attention(proj, vt, batch, seq)
            z_blk, w_out = 0, c_w_out[j]
        else:
            heads = [(2 * dm + hh * LANES, LANES, 0, LANES) for hh in range(D_HEADS // 2)]
            proj, vt = _proj_in(h, norm_pre[i], d_w_in[j].astype(BF16), batch, [heads])
            o = _stick_attention(proj, vt, batch, seq)
            z_blk, w_out = 3, d_w_out[j]
        h = _proj_out([o], proj, z_blk, w_out.astype(BF16), norm_post[i], h)
    return h.reshape(batch, seq, dm)
```
